```python
import math
import jax, jax.numpy as jnp
from jax import lax
import numpy as np

D_MODEL = 1024
BATCH = 2
SEQ = 8192
DEPTH = 1

RET_HEADS = 8
RET_QK_DIM = 128
RET_V_DIM = 256
RET_QK_WIDTH = RET_HEADS * RET_QK_DIM
RET_V_WIDTH = RET_HEADS * RET_V_DIM
CHUNK = 128
ROPE_BASE = 10000.0
CONV_DIM = D_MODEL
CONV_WIDTH = 31
CONV_HALF = CONV_WIDTH // 2
D_FF = 4 * D_MODEL
N_BRANCHES = 2
EPS = 1e-6
GN_EPS = 1e-5

SPLITS = np.cumsum([RET_QK_WIDTH, RET_QK_WIDTH, RET_V_WIDTH, RET_V_WIDTH, 2 * CONV_DIM])[:].tolist()
IN_WIDTH = 2 * RET_QK_WIDTH + 2 * RET_V_WIDTH + 2 * CONV_DIM + N_BRANCHES * D_MODEL

kernel_name = "hybrid_retention_conformer_block"


def rmsnorm(x, w):
    xf = x.astype(jnp.float32)
    y = xf * lax.rsqrt(jnp.mean(xf * xf, axis=-1, keepdims=True) + EPS)
    return (y * w.astype(jnp.float32)).astype(x.dtype)


def layernorm(x, w, b):
    xf = x.astype(jnp.float32)
    mu = jnp.mean(xf, axis=-1, keepdims=True)
    var = jnp.mean(jnp.square(xf - mu), axis=-1, keepdims=True)
    y = (xf - mu) * lax.rsqrt(var + GN_EPS)
    return (y * w.astype(jnp.float32) + b.astype(jnp.float32)).astype(x.dtype)


def rotary(t, cos, sin):
    t1, t2 = jnp.split(t, 2, axis=-1)
    return jnp.concatenate([t1 * cos - t2 * sin, t2 * cos + t1 * sin], axis=-1)


def retention_one_direction(q, k, v, log_gamma, strict):
    dt = q.dtype
    t = jnp.arange(CHUNK)
    diff = (t[:, None] - t[None, :])
    mask = diff > 0 if strict else diff >= 0
    lg = log_gamma.astype(jnp.float32)
    decay = jnp.where(mask[None], jnp.exp(lg[:, None, None] * jnp.maximum(diff, 0)[None].astype(jnp.float32)), 0.0)
    tf = t.astype(jnp.float32)
    xi = jnp.exp(lg[:, None] * (tf + 1.0)).astype(dt)
    zeta = jnp.exp(lg[:, None] * (CHUNK - 1.0 - tf)).astype(dt)
    g_chunk = jnp.exp(lg * CHUNK).astype(dt)
    scores = jnp.einsum('bhnid,bhnjd->bhnij', q, k) * decay.astype(dt)[None, :, None]
    inner = jnp.einsum('bhnij,bhnjv->bhniv', scores, v)
    upd = jnp.einsum('bhncd,bhnce->bhnde', k * zeta[None, :, None, :, None], v)
    upd = jnp.moveaxis(upd, 2, 0)

    def step(state, u):
        return g_chunk[None, :, None, None] * state + u, state

    _, prev_states = lax.scan(step, jnp.zeros_like(upd[0]), upd)
    prev_states = jnp.moveaxis(prev_states, 0, 2)
    cross = jnp.einsum('bhncd,bhnde->bhnce', q * xi[None, :, None, :, None], prev_states)
    return inner + cross


def bidirectional_retention(q, k, v, log_gamma_fwd, log_gamma_bwd):
    b, s, h, _ = q.shape
    n = s // CHUNK

    def to_chunks(t):
        return t.reshape(b, n, CHUNK, h, -1).transpose(0, 3, 1, 2, 4)

    def from_chunks(t):
        return t.transpose(0, 2, 3, 1, 4).reshape(b, s, h, -1)

    def flip(t):
        return t[:, ::-1]

    fwd = from_chunks(retention_one_direction(to_chunks(q), to_chunks(k), to_chunks(v), log_gamma_fwd, False))
    bwd = flip(from_chunks(retention_one_direction(to_chunks(flip(q)), to_chunks(flip(k)), to_chunks(flip(v)),
                                                   log_gamma_bwd, True)))
    return fwd + bwd


def head_groupnorm(o, w):
    of = o.astype(jnp.float32)
    mu = jnp.mean(of, axis=-1, keepdims=True)
    var = jnp.mean(jnp.square(of - mu), axis=-1, keepdims=True)
    y = ((of - mu) * lax.rsqrt(var + GN_EPS)).reshape(o.shape[0], o.shape[1], -1)
    return (y * w.astype(jnp.float32)).astype(o.dtype)


def setup_inputs(seed: int = 0) -> dict:
    key = jax.random.key(seed)
    ks = jax.random.split(key, 20)
    f32 = jnp.float32

    def nrm(k, shape, scale):
        return jax.random.normal(k, shape, f32) * scale

    def gain(k, n):
        return 1.0 + 0.02 * jax.random.normal(k, (DEPTH, n), f32)

    base = jnp.log(-jnp.log1p(-jnp.exp2(-5.0 - jnp.arange(RET_HEADS, dtype=f32))))
    ret_decay_raw = base[None, None, :] + 0.05 * jax.random.normal(ks[3], (DEPTH, 2, RET_HEADS), f32)
    return {
        "x": jax.random.normal(ks[0], (BATCH, SEQ, D_MODEL), f32),
        "norm1_w": gain(ks[1], D_MODEL),
        "w_in": nrm(ks[2], (DEPTH, D_MODEL, IN_WIDTH), D_MODEL ** -0.5),
        "ret_decay_raw": ret_decay_raw,
        "ret_gn_w": gain(ks[4], RET_V_WIDTH),
        "w_ret_o": nrm(ks[5], (DEPTH, RET_V_WIDTH, D_MODEL), RET_V_WIDTH ** -0.5),
        "b_glu": nrm(ks[6], (DEPTH, 2 * CONV_DIM), 0.02),
        "conv_w": nrm(ks[7], (DEPTH, CONV_WIDTH, CONV_DIM), CONV_WIDTH ** -0.5),
        "conv_b": nrm(ks[8], (DEPTH, CONV_DIM), 0.02),
        "conv_ln_w": gain(ks[9], CONV_DIM),
        "conv_ln_b": nrm(ks[10], (DEPTH, CONV_DIM), 0.02),
        "w_conv_o": nrm(ks[11], (DEPTH, CONV_DIM, D_MODEL), CONV_DIM ** -0.5),
        "b_conv_o": nrm(ks[12], (DEPTH, D_MODEL), 0.02),
        "w_out": nrm(ks[13], (DEPTH, D_MODEL, D_MODEL), D_MODEL ** -0.5),
        "norm2_w": gain(ks[14], D_MODEL),
        "w_mlp1": nrm(ks[15], (DEPTH, D_MODEL, D_FF), D_MODEL ** -0.5),
        "w_mlp2": nrm(ks[16], (DEPTH, D_FF, D_MODEL), D_FF ** -0.5),
        "norm_f_w": 1.0 + 0.02 * jax.random.normal(ks[17], (D_MODEL,), f32),
    }


def reference(x, norm1_w, w_in, ret_decay_raw, ret_gn_w, w_ret_o, b_glu, conv_w, conv_b, conv_ln_w,
              conv_ln_b, w_conv_o, b_conv_o, w_out, norm2_w, w_mlp1, w_mlp2, norm_f_w):
    b, s, _ = x.shape
    inv_freq = ROPE_BASE ** (-jnp.arange(0, RET_QK_DIM, 2, dtype=jnp.float32) / RET_QK_DIM)
    ang = jnp.arange(s, dtype=jnp.float32)[:, None] * inv_freq[None, :]
    cos = jnp.cos(ang)[:, None, :].astype(x.dtype)
    sin = jnp.sin(ang)[:, None, :].astype(x.dtype)
    k_scale = RET_QK_DIM ** -0.5

    for l in range(DEPTH):
        h = rmsnorm(x, norm1_w[l])
        proj = h @ w_in[l]
        q, k, v, g, glu, gates = jnp.split(proj, SPLITS, axis=-1)
        q = rotary(q.reshape(b, s, RET_HEADS, RET_QK_DIM), cos, sin)
        k = rotary(k.reshape(b, s, RET_HEADS, RET_QK_DIM), cos, sin) * k_scale
        v = v.reshape(b, s, RET_HEADS, RET_V_DIM)
        log_gamma = -jnp.exp(ret_decay_raw[l].astype(jnp.float32))
        o = bidirectional_retention(q, k, v, log_gamma[0], log_gamma[1])
        o = head_groupnorm(o, ret_gn_w[l]) * jax.nn.silu(g)
        y_ret = o @ w_ret_o[l]
        glu = glu + b_glu[l]
        ga, gb = jnp.split(glu, 2, axis=-1)
        u = ga * jax.nn.sigmoid(gb)
        u = lax.conv_general_dilated(u, conv_w[l][:, None, :].astype(u.dtype), window_strides=(1,),
                                     padding=[(CONV_HALF, CONV_HALF)],
                                     dimension_numbers=('NWC', 'WIO', 'NWC'),
                                     feature_group_count=CONV_DIM) + conv_b[l]
        u = jax.nn.silu(layernorm(u, conv_ln_w[l], conv_ln_b[l]))
        y_conv = u @ w_conv_o[l] + b_conv_o[l]
        gate_ret, gate_conv = jnp.split(jax.nn.sigmoid(gates), N_BRANCHES, axis=-1)
        x = x + (gate_ret * y_ret + gate_conv * y_conv) @ w_out[l]
        h = rmsnorm(x, norm2_w[l])
        x = x + jnp.square(jax.nn.relu(h @ w_mlp1[l])) @ w_mlp2[l]
    return rmsnorm(x, norm_f_w)
```

```python
import functools
import math

import jax
import jax.numpy as jnp
from jax import lax
from jax.experimental import pallas as pl
from jax.experimental.pallas import tpu as pltpu

F32 = jnp.float32
BF16 = jnp.bfloat16

RET_HEADS = 8
RET_QK_DIM = 128
RET_V_DIM = 256
CONV_WIDTH = 31
CONV_HALF = CONV_WIDTH // 2
ROPE_BASE = 10000.0
EPS = 1e-6
GN_EPS = 1e-5

LANES = 128
SUBLANES = 8
HALO = 16
RET_CHUNK = 256
VMEM_LIMIT = 56 * 1024 * 1024


def _sigmoid(x):
    return 1.0 / (1.0 + jnp.exp(-x))


def _rms_scale(x):
    return x * lax.rsqrt(jnp.mean(x * x, axis=-1, keepdims=True) + EPS)


def _in_proj_kernel(x_ref, n1_ref, wa_ref, wb_ref, cos_ref, sin_ref, bga_ref, bgb_ref,
                    proj_ref, u_ref, h_ref, *, k_scale):
    j = pl.program_id(1)

    @pl.when(j == 0)
    def _():
        h_ref[...] = (_rms_scale(x_ref[...]) * n1_ref[...]).astype(BF16)

    def dots():
        h = h_ref[...]
        a = jnp.dot(h, wa_ref[...], preferred_element_type=F32)
        b = jnp.dot(h, wb_ref[...], preferred_element_type=F32)
        return a, b

    @pl.when(j < 2)
    def _():
        scale = jnp.where(j == 0, 1.0, k_scale).astype(F32)
        c = cos_ref[...] * scale
        s = sin_ref[...] * scale
        for half, r in enumerate(dots()):
            for hh in range(512 // RET_QK_DIM):
                t = r[:, hh * RET_QK_DIM:(hh + 1) * RET_QK_DIM]
                rot = pltpu.roll(t, RET_QK_DIM // 2, axis=1)
                lo = half * 512 + hh * RET_QK_DIM
                proj_ref[:, lo:lo + RET_QK_DIM] = (t * c + rot * s).astype(BF16)

    @pl.when((j >= 2) & (j < 4))
    def _():
        a, b = dots()
        proj_ref[:, :512] = a.astype(BF16)
        proj_ref[:, 512:] = b.astype(BF16)

    @pl.when((j >= 4) & (j < 6))
    def _():
        a, b = dots()
        proj_ref[:, :512] = (a * _sigmoid(a)).astype(BF16)
        proj_ref[:, 512:] = (b * _sigmoid(b)).astype(BF16)

    for half in range(2):
        @pl.when(j == 6 + half)
        def _(half=half):
            a, b = dots()
            u = (a + bga_ref[...]) * _sigmoid(b + bgb_ref[...])
            for g in range(512 // LANES):
                u_ref[pl.ds(half * 4 + g, u.shape[0], stride=SUBLANES), :] = (
                    u[:, g * LANES:(g + 1) * LANES])

    @pl.when(j >= 8)
    def _():
        a, b = dots()
        proj_ref[:, :512] = _sigmoid(a).astype(BF16)
        proj_ref[:, 512:] = _sigmoid(b).astype(BF16)


def _in_proj(x2d, n1, w_in, cos2, sin2, b_glu, seq, tm=1024):
    m, d = x2d.shape
    sblocks = seq // tm

    def wa_map(i, j):
        return (0, jnp.where(j == 6, 12, jnp.where(j == 7, 13, 2 * j)))

    def wb_map(i, j):
        return (0, jnp.where(j == 6, 14, jnp.where(j == 7, 15, 2 * j + 1)))

    def proj_map(i, j):
        return (i, jnp.where(j < 6, j, jnp.where(j < 8, 5, j - 2)))

    return pl.pallas_call(
        functools.partial(_in_proj_kernel, k_scale=RET_QK_DIM ** -0.5),
        grid=(m // tm, 10),
        in_specs=[
            pl.BlockSpec((tm, d), lambda i, j: (i, 0)),
            pl.BlockSpec((1, d), lambda i, j: (0, 0)),
            pl.BlockSpec((d, 512), wa_map),
            pl.BlockSpec((d, 512), wb_map),
            pl.BlockSpec((tm, RET_QK_DIM), lambda i, j: (i % sblocks, 0)),
            pl.BlockSpec((tm, RET_QK_DIM), lambda i, j: (i % sblocks, 0)),
            pl.BlockSpec((1, 512), lambda i, j: (0, jnp.clip(j - 6, 0, 1))),
            pl.BlockSpec((1, 512), lambda i, j: (0, 2 + jnp.clip(j - 6, 0, 1))),
        ],
        out_specs=[
            pl.BlockSpec((tm, 1024), proj_map),
            pl.BlockSpec((tm * SUBLANES, LANES), lambda i, j: (i, 0)),
        ],
        out_shape=[
            jax.ShapeDtypeStruct((m, 8192), BF16),
            jax.ShapeDtypeStruct((m * SUBLANES, LANES), F32),
        ],
        scratch_shapes=[pltpu.VMEM((tm, d), BF16)],
        compiler_params=pltpu.CompilerParams(
            dimension_semantics=("arbitrary", "arbitrary"),
            vmem_limit_bytes=VMEM_LIMIT),
        name="in_proj",
    )(x2d, n1, w_in, w_in, cos2, sin2, b_glu, b_glu)


def _ret_kernel(raw_ref, gnw_ref, q_ref, k_ref, v_ref, sg_ref, o_ref,
                rst_ref, d_ref, sc_ref):
    c = RET_CHUNK
    nchunks = q_ref.shape[1] // c
    lg = -jnp.exp(raw_ref[0])
    lgf = lg[0:1, :]
    lgb = lg[1:2, :]

    row = lax.broadcasted_iota(jnp.int32, (c, c), 0)
    col = lax.broadcasted_iota(jnp.int32, (c, c), 1)
    dpos = jnp.maximum(row - col, 0).astype(F32)
    dneg = jnp.maximum(col - row, 0).astype(F32)
    d_ref[...] = jnp.where(row >= col, jnp.exp(lgf * dpos), jnp.exp(lgb * dneg))

    pos = lax.broadcasted_iota(jnp.int32, (c, RET_QK_DIM), 0).astype(F32)
    lgf_k = lgf[:, :RET_QK_DIM]
    lgb_k = lgb[:, :RET_QK_DIM]
    sc_ref[0] = jnp.exp(lgf_k * (c - 1.0 - pos))
    sc_ref[1] = jnp.exp(lgb_k * pos)
    sc_ref[2] = jnp.exp(lgf_k * (pos + 1.0))
    sc_ref[3] = jnp.exp(lgb_k * (c - pos))
    g_f = jnp.exp(lgf * c)
    g_b = jnp.exp(lgb * c)

    tdims = (((0,), (0,)), ((), ()))

    def state_body(t, carry):
        rf, rb = carry
        nf = t
        nb = nchunks - 1 - t
        rst_ref[nf, 0:RET_QK_DIM, :] = rf.astype(BF16)
        rst_ref[nb, RET_QK_DIM:, :] = rb.astype(BF16)
        sf = pl.ds(pl.multiple_of(nf * c, c), c)
        sb = pl.ds(pl.multiple_of(nb * c, c), c)
        kzf = (k_ref[0, sf, :].astype(F32) * sc_ref[0]).astype(BF16)
        kzb = (k_ref[0, sb, :].astype(F32) * sc_ref[1]).astype(BF16)
        uf = lax.dot_general(kzf, v_ref[0, sf, :], tdims, preferred_element_type=F32)
        ub = lax.dot_general(kzb, v_ref[0, sb, :], tdims, preferred_element_type=F32)
        return g_f * rf + uf, g_b * rb + ub

    zero = jnp.zeros((RET_QK_DIM, RET_V_DIM), F32)
    lax.fori_loop(0, nchunks, state_body, (zero, zero))

    gnw = gnw_ref[...]

    def out_body(n, _):
        sl = pl.ds(pl.multiple_of(n * c, c), c)
        qc = q_ref[0, sl, :]
        s = lax.dot_general(qc, k_ref[0, sl, :], (((1,), (1,)), ((), ())),
                            preferred_element_type=F32)
        p = (s * d_ref[...]).astype(BF16)
        qf = qc.astype(F32)
        qx = jnp.concatenate([(qf * sc_ref[2]).astype(BF16),
                              (qf * sc_ref[3]).astype(BF16)], axis=1)
        o = (jnp.dot(p, v_ref[0, sl, :], preferred_element_type=F32)
             + jnp.dot(qx, rst_ref[n], preferred_element_type=F32))
        mu = jnp.mean(o, axis=-1, keepdims=True)
        oc = o - mu
        var = jnp.mean(oc * oc, axis=-1, keepdims=True)
        y = oc * lax.rsqrt(var + GN_EPS) * gnw * sg_ref[0, sl, :].astype(F32)
        o_ref[0, sl, :] = y.astype(BF16)
        return 0

    lax.fori_loop(0, nchunks, out_body, 0)


def _retention(proj3, raw_b, gnw):
    b, s, _ = proj3.shape
    h = RET_HEADS
    nchunks = s // RET_CHUNK
    return pl.pallas_call(
        _ret_kernel,
        grid=(b, h),
        in_specs=[
            pl.BlockSpec((1, 2, RET_V_DIM), lambda bi, hi: (hi, 0, 0)),
            pl.BlockSpec((1, RET_V_DIM), lambda bi, hi: (0, hi)),
            pl.BlockSpec((1, s, RET_QK_DIM), lambda bi, hi: (bi, 0, hi)),
            pl.BlockSpec((1, s, RET_QK_DIM), lambda bi, hi: (bi, 0, h + hi)),
            pl.BlockSpec((1, s, RET_V_DIM), lambda bi, hi: (bi, 0, h + hi)),
            pl.BlockSpec((1, s, RET_V_DIM), lambda bi, hi: (bi, 0, 2 * h + hi)),
        ],
        out_specs=pl.BlockSpec((1, s, RET_V_DIM), lambda bi, hi: (bi, 0, hi)),
        out_shape=jax.ShapeDtypeStruct((b, s, h * RET_V_DIM), BF16),
        scratch_shapes=[
            pltpu.VMEM((nchunks, 2 * RET_QK_DIM, RET_V_DIM), BF16),
            pltpu.VMEM((RET_CHUNK, RET_CHUNK), F32),
            pltpu.VMEM((4, RET_CHUNK, RET_QK_DIM), F32),
        ],
        compiler_params=pltpu.CompilerParams(
            dimension_semantics=("arbitrary", "arbitrary"),
            vmem_limit_bytes=VMEM_LIMIT),
        name="retention",
    )(raw_b, gnw, proj3, proj3, proj3, proj3)


def _conv_kernel(prev_ref, main_ref, next_ref, cw_ref, cb_ref, lnw_ref, lnb_ref,
                 out_ref, ubuf_ref, acc_ref, *, tblk, ln_rows):
    i = pl.program_id(1)
    last = pl.num_programs(1) - 1
    sl = SUBLANES
    ts = main_ref.shape[0] // sl
    ubuf_ref[0:HALO * sl, :] = jnp.where(i > 0, prev_ref[...], 0.0)
    ubuf_ref[HALO * sl:(HALO + ts) * sl, :] = main_ref[...]
    ubuf_ref[(HALO + ts) * sl:, :] = jnp.where(i < last, next_ref[...], 0.0)

    w = [cw_ref[k * sl:(k + 1) * sl, :] for k in range(CONV_WIDTH)]
    bias = cb_ref[...]

    def conv_body(bi, _):
        t0 = bi * tblk
        accs = [bias] * tblk
        for m in range(tblk + CONV_WIDTH - 1):
            start = pl.multiple_of((t0 + (HALO - CONV_HALF) + m) * sl, sl)
            tok = ubuf_ref[pl.ds(start, sl), :]
            for tt in range(tblk):
                k = m - tt
                if 0 <= k < CONV_WIDTH:
                    accs[tt] = accs[tt] + tok * w[k]
        for tt in range(tblk):
            acc_ref[pl.ds(pl.multiple_of((t0 + tt) * sl, sl), sl), :] = accs[tt]
        return 0

    lax.fori_loop(0, ts // tblk, conv_body, 0)

    lnw = lnw_ref[...]
    lnb = lnb_ref[...]

    def ln_body(ri, _):
        r0 = pl.multiple_of(ri * ln_rows, ln_rows)
        a = jnp.concatenate(
            [acc_ref[pl.ds(r0 * sl + g, ln_rows, stride=sl), :] for g in range(sl)], axis=1)
        mu = jnp.mean(a, axis=-1, keepdims=True)
        ac = a - mu
        var = jnp.mean(ac * ac, axis=-1, keepdims=True)
        y = ac * lax.rsqrt(var + GN_EPS) * lnw + lnb
        out_ref[0, pl.ds(r0, ln_rows), :] = (y * _sigmoid(y)).astype(BF16)
        return 0

    lax.fori_loop(0, ts // ln_rows, ln_body, 0)


def _conv_branch(u_tm, conv_w, conv_b, ln_w, ln_b, batch, seq, ts=512):
    c = conv_w.shape[1]
    sl = SUBLANES
    nblk = seq // ts
    hb = ts // HALO
    nhalo = batch * seq // HALO
    cw = conv_w.reshape(CONV_WIDTH * sl, LANES)
    cb = conv_b.reshape(sl, LANES)
    return pl.pallas_call(
        functools.partial(_conv_kernel, tblk=8, ln_rows=64),
        grid=(batch, nblk),
        in_specs=[
            pl.BlockSpec((HALO * sl, LANES),
                         lambda bi, i: (jnp.maximum((bi * nblk + i) * hb - 1, 0), 0)),
            pl.BlockSpec((ts * sl, LANES), lambda bi, i: (bi * nblk + i, 0)),
            pl.BlockSpec((HALO * sl, LANES),
                         lambda bi, i: (jnp.minimum((bi * nblk + i + 1) * hb, nhalo - 1), 0)),
            pl.BlockSpec((CONV_WIDTH * sl, LANES), lambda bi, i: (0, 0)),
            pl.BlockSpec((sl, LANES), lambda bi, i: (0, 0)),
            pl.BlockSpec((1, c), lambda bi, i: (0, 0)),
            pl.BlockSpec((1, c), lambda bi, i: (0, 0)),
        ],
        out_specs=pl.BlockSpec((1, ts, c), lambda bi, i: (bi, i, 0)),
        out_shape=jax.ShapeDtypeStruct((batch, seq, c), BF16),
        scratch_shapes=[
            pltpu.VMEM(((ts + 2 * HALO) * sl, LANES), F32),
            pltpu.VMEM((ts * sl, LANES), F32),
        ],
        compiler_params=pltpu.CompilerParams(
            dimension_semantics=("arbitrary", "arbitrary"),
            vmem_limit_bytes=VMEM_LIMIT),
        name="conv_branch",
    )(u_tm, u_tm, u_tm, cw, cb, ln_w, ln_b)


def _merge_kernel(x_ref, og_ref, uc_ref, gr_ref, gc_ref, wro_ref, wco_ref, bco_ref,
                  wout_ref, x2_ref):
    y_ret = jnp.dot(og_ref[...], wro_ref[...], preferred_element_type=F32)
    y_conv = jnp.dot(uc_ref[...], wco_ref[...], preferred_element_type=F32) + bco_ref[...]
    m = gr_ref[...].astype(F32) * y_ret + gc_ref[...].astype(F32) * y_conv
    x2_ref[...] = x_ref[...] + jnp.dot(m.astype(BF16), wout_ref[...],
                                       preferred_element_type=F32)


def _const_spec(shape):
    return pl.BlockSpec(shape, lambda i: (0,) * len(shape), pipeline_mode=pl.Buffered(1))


def _merge(x2d, og2d, uc2d, proj2d, w_ret_o, w_conv_o, b_conv_o, w_out, tm=512):
    m, d = x2d.shape
    vw = og2d.shape[1]
    return pl.pallas_call(
        _merge_kernel,
        grid=(m // tm,),
        in_specs=[
            pl.BlockSpec((tm, d), lambda i: (i, 0)),
            pl.BlockSpec((tm, vw), lambda i: (i, 0)),
            pl.BlockSpec((tm, d), lambda i: (i, 0)),
            pl.BlockSpec((tm, d), lambda i: (i, 6)),
            pl.BlockSpec((tm, d), lambda i: (i, 7)),
            _const_spec((vw, d)),
            _const_spec((d, d)),
            _const_spec((1, d)),
            _const_spec((d, d)),
        ],
        out_specs=pl.BlockSpec((tm, d), lambda i: (i, 0)),
        out_shape=jax.ShapeDtypeStruct((m, d), F32),
        compiler_params=pltpu.CompilerParams(
            dimension_semantics=("arbitrary",),
            vmem_limit_bytes=VMEM_LIMIT),
        name="merge",
    )(x2d, og2d, uc2d, proj2d, proj2d, w_ret_o, w_conv_o, b_conv_o, w_out)


def _mlp_kernel(x_ref, n2_ref, w1_ref, w2_ref, nf_ref, out_ref, *, final_norm, ff_chunk):
    x = x_ref[...]
    h = (_rms_scale(x) * n2_ref[...]).astype(BF16)
    acc = x
    for c0 in range(0, w1_ref.shape[1], ff_chunk):
        a = jnp.dot(h, w1_ref[:, c0:c0 + ff_chunk], preferred_element_type=F32)
        a = jnp.square(jnp.maximum(a, 0.0)).astype(BF16)
        acc = acc + jnp.dot(a, w2_ref[c0:c0 + ff_chunk, :], preferred_element_type=F32)
    if final_norm:
        acc = _rms_scale(acc) * nf_ref[...]
    out_ref[...] = acc


def _mlp(x2d, n2, w1, w2, nf, final_norm, tm=512):
    m, d = x2d.shape
    dff = w1.shape[1]
    return pl.pallas_call(
        functools.partial(_mlp_kernel, final_norm=final_norm, ff_chunk=1024),
        grid=(m // tm,),
        in_specs=[
            pl.BlockSpec((tm, d), lambda i: (i, 0)),
            _const_spec((1, d)),
            _const_spec((d, dff)),
            _const_spec((dff, d)),
            _const_spec((1, d)),
        ],
        out_specs=pl.BlockSpec((tm, d), lambda i: (i, 0)),
        out_shape=jax.ShapeDtypeStruct((m, d), F32),
        compiler_params=pltpu.CompilerParams(
            dimension_semantics=("arbitrary",),
            vmem_limit_bytes=VMEM_LIMIT),
        name="mlp",
    )(x2d, n2, w1, w2, nf)


def kernel(x, norm1_w, w_in, ret_decay_raw, ret_gn_w, w_ret_o, b_glu, conv_w, conv_b, conv_ln_w,
           conv_ln_b, w_conv_o, b_conv_o, w_out, norm2_w, w_mlp1, w_mlp2, norm_f_w):
    b, s, d = x.shape
    depth = w_in.shape[0]

    inv_freq = ROPE_BASE ** (-jnp.arange(0, RET_QK_DIM, 2, dtype=F32) / RET_QK_DIM)
    ang = jnp.arange(s, dtype=F32)[:, None] * inv_freq[None, :]
    cos2 = jnp.concatenate([jnp.cos(ang), jnp.cos(ang)], axis=-1)
    sin2 = jnp.concatenate([-jnp.sin(ang), jnp.sin(ang)], axis=-1)

    x2d = x.reshape(b * s, d)
    for l in range(depth):
        proj, u = _in_proj(x2d, norm1_w[l][None, :], w_in[l].astype(BF16), cos2, sin2,
                           b_glu[l][None, :], s)
        raw_b = jnp.broadcast_to(ret_decay_raw[l].T[:, :, None], (RET_HEADS, 2, RET_V_DIM))
        og = _retention(proj.reshape(b, s, -1), raw_b, ret_gn_w[l][None, :])
        uc = _conv_branch(u, conv_w[l], conv_b[l], conv_ln_w[l][None, :],
                          conv_ln_b[l][None, :], b, s)
        x2d = _merge(x2d, og.reshape(b * s, -1), uc.reshape(b * s, -1), proj,
                     w_ret_o[l].astype(BF16), w_conv_o[l].astype(BF16), b_conv_o[l][None, :],
                     w_out[l].astype(BF16))
        x2d = _mlp(x2d, norm2_w[l][None, :], w_mlp1[l].astype(BF16), w_mlp2[l].astype(BF16),
                   norm_f_w[None, :], final_norm=(l == depth - 1))
    return x2d.reshape(b, s, d)
```

```python
import functools

import jax
import jax.numpy as jnp
from jax import lax
from jax.experimental import pallas as pl
from jax.experimental.pallas import tpu as pltpu

F32 = jnp.float32
BF16 = jnp.bfloat16

RET_HEADS = 8
RET_QK_DIM = 128
RET_V_DIM = 256
CONV_WIDTH = 31
CONV_HALF = CONV_WIDTH // 2
ROPE_BASE = 10000.0
EPS = 1e-6
GN_EPS = 1e-5

LANES = 128
SUBLANES = 8
HALO = 16
RET_CHUNK = 256
VMEM_LIMIT = 56 * 1024 * 1024


def _sigmoid(x):
    return 1.0 / (1.0 + jnp.exp(-x))


def _rms_scale(x):
    return x * lax.rsqrt(jnp.mean(x * x, axis=-1, keepdims=True) + EPS)


def _in_proj_kernel(x_ref, n1_ref, wa_ref, wb_ref, cos_ref, sin_ref, bglu_ref, gnw_ref,
                    proj_ref, u_ref, h_ref, *, k_scale):
    j = pl.program_id(1)
    nw = wa_ref.shape[1]

    @pl.when(j == 0)
    def _():
        h_ref[...] = (_rms_scale(x_ref[...]) * n1_ref[...]).astype(BF16)

    def dots():
        h = h_ref[...]
        a = jnp.dot(h, wa_ref[...], preferred_element_type=F32)
        b = jnp.dot(h, wb_ref[...], preferred_element_type=F32)
        return a, b

    @pl.when(j == 0)
    def _():
        a, b = dots()
        c = cos_ref[...]
        s = sin_ref[...]
        for idx, (r, cc, ss) in enumerate(((a, c, s), (b, c * k_scale, s * k_scale))):
            for hh in range(nw // RET_QK_DIM):
                t = r[:, hh * RET_QK_DIM:(hh + 1) * RET_QK_DIM]
                rot = pltpu.roll(t, RET_QK_DIM // 2, axis=1)
                lo = idx * nw + hh * RET_QK_DIM
                proj_ref[:, lo:lo + RET_QK_DIM] = (t * cc + rot * ss).astype(BF16)

    @pl.when(j == 1)
    def _():
        a, b = dots()
        proj_ref[:, :nw] = a.astype(BF16)
        proj_ref[:, nw:] = b.astype(BF16)

    @pl.when(j == 2)
    def _():
        a, b = dots()
        proj_ref[:, :nw] = (a * _sigmoid(a) * gnw_ref[:, :nw]).astype(BF16)
        proj_ref[:, nw:] = (b * _sigmoid(b) * gnw_ref[:, nw:]).astype(BF16)

    @pl.when(j == 3)
    def _():
        a, b = dots()
        u = (a + bglu_ref[:, :nw]) * _sigmoid(b + bglu_ref[:, nw:])
        for g in range(nw // LANES):
            u_ref[pl.ds(g, u.shape[0], stride=SUBLANES), :] = u[:, g * LANES:(g + 1) * LANES]

    @pl.when(j == 4)
    def _():
        a, b = dots()
        proj_ref[:, :nw] = _sigmoid(a).astype(BF16)
        proj_ref[:, nw:] = _sigmoid(b).astype(BF16)


def _in_proj(x2d, n1, w_in, cos2, sin2, b_glu, gnw, seq, tm=1024):
    m, d = x2d.shape
    sblocks = seq // tm
    nw = 1024
    assert nw // LANES == SUBLANES

    return pl.pallas_call(
        functools.partial(_in_proj_kernel, k_scale=RET_QK_DIM ** -0.5),
        grid=(m // tm, 5),
        in_specs=[
            pl.BlockSpec((tm, d), lambda i, j: (i, 0)),
            pl.BlockSpec((1, d), lambda i, j: (0, 0)),
            pl.BlockSpec((d, nw), lambda i, j: (0, 2 * j)),
            pl.BlockSpec((d, nw), lambda i, j: (0, 2 * j + 1)),
            pl.BlockSpec((tm, RET_QK_DIM), lambda i, j: (i % sblocks, 0)),
            pl.BlockSpec((tm, RET_QK_DIM), lambda i, j: (i % sblocks, 0)),
            pl.BlockSpec((1, 2 * nw), lambda i, j: (0, 0)),
            pl.BlockSpec((1, 2 * nw), lambda i, j: (0, 0)),
        ],
        out_specs=[
            pl.BlockSpec((tm, 2 * nw), lambda i, j: (i, jnp.where(j < 3, j, j - 1))),
            pl.BlockSpec((tm * SUBLANES, LANES), lambda i, j: (i, 0)),
        ],
        out_shape=[
            jax.ShapeDtypeStruct((m, 8 * nw), BF16),
            jax.ShapeDtypeStruct((m * SUBLANES, LANES), F32),
        ],
        scratch_shapes=[pltpu.VMEM((tm, d), BF16)],
        compiler_params=pltpu.CompilerParams(
            dimension_semantics=("arbitrary", "arbitrary"),
            vmem_limit_bytes=VMEM_LIMIT),
        name="in_proj",
    )(x2d, n1, w_in, w_in, cos2, sin2, b_glu, gnw)


def _ret_kernel(raw_ref, q_ref, k_ref, v_ref, sg_ref, o_ref,
                rst_ref, d_ref, zeta_ref, xi_ref, *, unroll):
    c = RET_CHUNK
    dk = RET_QK_DIM
    nchunks = q_ref.shape[1] // c
    lg = -jnp.exp(raw_ref[0])
    lgf = lg[0:1, :]
    lgb = lg[1:2, :]

    row = lax.broadcasted_iota(jnp.int32, (c, c), 0)
    col = lax.broadcasted_iota(jnp.int32, (c, c), 1)
    dpos = jnp.maximum(row - col, 0).astype(F32)
    dneg = jnp.maximum(col - row, 0).astype(F32)
    d_ref[...] = jnp.where(row >= col, jnp.exp(lgf * dpos), jnp.exp(lgb * dneg))

    pos = lax.broadcasted_iota(jnp.int32, (c, dk), 0).astype(F32)
    lgf_k = lgf[:, :dk]
    lgb_k = lgb[:, :dk]
    zeta_ref[0] = jnp.exp(lgf_k * (c - 1.0 - pos)).astype(BF16)
    zeta_ref[1] = jnp.exp(lgb_k * pos).astype(BF16)
    xi_ref[:, :dk] = jnp.exp(lgf_k * (pos + 1.0)).astype(BF16)
    xi_ref[:, dk:] = jnp.exp(lgb_k * (c - pos)).astype(BF16)
    g_f = jnp.exp(lgf * c)
    g_b = jnp.exp(lgb * c)

    tdims = (((0,), (0,)), ((), ()))

    def state_body(t, carry):
        rf, rb = carry
        nf = t
        nb = nchunks - 1 - t
        rst_ref[nf, 0:dk, :] = rf.astype(BF16)
        rst_ref[nb, dk:, :] = rb.astype(BF16)
        sf = pl.ds(pl.multiple_of(nf * c, c), c)
        sb = pl.ds(pl.multiple_of(nb * c, c), c)
        kzf = k_ref[0, sf, :] * zeta_ref[0]
        kzb = k_ref[0, sb, :] * zeta_ref[1]
        uf = lax.dot_general(kzf, v_ref[0, sf, :], tdims, preferred_element_type=F32)
        ub = lax.dot_general(kzb, v_ref[0, sb, :], tdims, preferred_element_type=F32)
        return g_f * rf + uf, g_b * rb + ub

    zero = jnp.zeros((dk, RET_V_DIM), F32)
    lax.fori_loop(0, nchunks, state_body, (zero, zero), unroll=unroll)

    def out_body(n, _):
        sl = pl.ds(pl.multiple_of(n * c, c), c)
        qc = q_ref[0, sl, :]
        s = lax.dot_general(qc, k_ref[0, sl, :], (((1,), (1,)), ((), ())),
                            preferred_element_type=F32)
        p = (s * d_ref[...]).astype(BF16)
        qx = jnp.concatenate([qc, qc], axis=1) * xi_ref[...]
        o = (jnp.dot(p, v_ref[0, sl, :], preferred_element_type=F32)
             + jnp.dot(qx, rst_ref[n], preferred_element_type=F32))
        mu = jnp.mean(o, axis=-1, keepdims=True)
        oc = o - mu
        var = jnp.mean(oc * oc, axis=-1, keepdims=True)
        o_ref[0, sl, :] = (oc * lax.rsqrt(var + GN_EPS)).astype(BF16) * sg_ref[0, sl, :]
        return 0

    lax.fori_loop(0, nchunks, out_body, 0, unroll=unroll)


def _retention(proj3, raw_b):
    b, s, _ = proj3.shape
    h = RET_HEADS
    nchunks = s // RET_CHUNK
    return pl.pallas_call(
        functools.partial(_ret_kernel, unroll=8),
        grid=(b, h),
        in_specs=[
            pl.BlockSpec((1, 2, RET_V_DIM), lambda bi, hi: (hi, 0, 0)),
            pl.BlockSpec((1, s, RET_QK_DIM), lambda bi, hi: (bi, 0, hi)),
            pl.BlockSpec((1, s, RET_QK_DIM), lambda bi, hi: (bi, 0, h + hi)),
            pl.BlockSpec((1, s, RET_V_DIM), lambda bi, hi: (bi, 0, h + hi)),
            pl.BlockSpec((1, s, RET_V_DIM), lambda bi, hi: (bi, 0, 2 * h + hi)),
        ],
        out_specs=pl.BlockSpec((1, s, RET_V_DIM), lambda bi, hi: (bi, 0, hi)),
        out_shape=jax.ShapeDtypeStruct((b, s, h * RET_V_DIM), BF16),
        scratch_shapes=[
            pltpu.VMEM((nchunks, 2 * RET_QK_DIM, RET_V_DIM), BF16),
            pltpu.VMEM((RET_CHUNK, RET_CHUNK), F32),
            pltpu.VMEM((2, RET_CHUNK, RET_QK_DIM), BF16),
            pltpu.VMEM((RET_CHUNK, 2 * RET_QK_DIM), BF16),
        ],
        compiler_params=pltpu.CompilerParams(
            dimension_semantics=("arbitrary", "arbitrary"),
            vmem_limit_bytes=VMEM_LIMIT),
        name="retention",
    )(raw_b, proj3, proj3, proj3, proj3)


def _conv_kernel(prev_ref, main_ref, next_ref, cw_ref, cb_ref, lnw_ref, lnb_ref,
                 out_ref, ubuf_ref, acc_ref, *, tblk, ln_rows):
    i = pl.program_id(1)
    last = pl.num_programs(1) - 1
    sl = SUBLANES
    ts = main_ref.shape[0] // sl
    ubuf_ref[0:HALO * sl, :] = jnp.where(i > 0, prev_ref[...], 0.0)
    ubuf_ref[HALO * sl:(HALO + ts) * sl, :] = main_ref[...]
    ubuf_ref[(HALO + ts) * sl:, :] = jnp.where(i < last, next_ref[...], 0.0)

    w = [cw_ref[k * sl:(k + 1) * sl, :] for k in range(CONV_WIDTH)]
    bias = cb_ref[...]

    def conv_body(bi, _):
        t0 = bi * tblk
        accs = [bias] * tblk
        for m in range(tblk + CONV_WIDTH - 1):
            start = pl.multiple_of((t0 + (HALO - CONV_HALF) + m) * sl, sl)
            tok = ubuf_ref[pl.ds(start, sl), :]
            for tt in range(tblk):
                k = m - tt
                if 0 <= k < CONV_WIDTH:
                    accs[tt] = accs[tt] + tok * w[k]
        for tt in range(tblk):
            acc_ref[pl.ds(pl.multiple_of((t0 + tt) * sl, sl), sl), :] = accs[tt]
        return 0

    lax.fori_loop(0, ts // tblk, conv_body, 0)

    lnw = lnw_ref[...]
    lnb = lnb_ref[...]

    def ln_body(ri, _):
        r0 = pl.multiple_of(ri * ln_rows, ln_rows)
        a = jnp.concatenate(
            [acc_ref[pl.ds(r0 * sl + g, ln_rows, stride=sl), :] for g in range(sl)], axis=1)
        mu = jnp.mean(a, axis=-1, keepdims=True)
        ac = a - mu
        var = jnp.mean(ac * ac, axis=-1, keepdims=True)
        y = ac * lax.rsqrt(var + GN_EPS) * lnw + lnb
        out_ref[0, pl.ds(r0, ln_rows), :] = (y * _sigmoid(y)).astype(BF16)
        return 0

    lax.fori_loop(0, ts // ln_rows, ln_body, 0, unroll=2)


def _conv_branch(u_tm, conv_w, conv_b, ln_w, ln_b, batch, seq, ts=512):
    c = conv_w.shape[1]
    sl = SUBLANES
    nblk = seq // ts
    hb = ts // HALO
    nhalo = batch * seq // HALO
    cw = conv_w.reshape(CONV_WIDTH * sl, LANES)
    cb = conv_b.reshape(sl, LANES)
    return pl.pallas_call(
        functools.partial(_conv_kernel, tblk=16, ln_rows=64),
        grid=(batch, nblk),
        in_specs=[
            pl.BlockSpec((HALO * sl, LANES),
                         lambda bi, i: (jnp.maximum((bi * nblk + i) * hb - 1, 0), 0)),
            pl.BlockSpec((ts * sl, LANES), lambda bi, i: (bi * nblk + i, 0)),
            pl.BlockSpec((HALO * sl, LANES),
                         lambda bi, i: (jnp.minimum((bi * nblk + i + 1) * hb, nhalo - 1), 0)),
            pl.BlockSpec((CONV_WIDTH * sl, LANES), lambda bi, i: (0, 0)),
            pl.BlockSpec((sl, LANES), lambda bi, i: (0, 0)),
            pl.BlockSpec((1, c), lambda bi, i: (0, 0)),
            pl.BlockSpec((1, c), lambda bi, i: (0, 0)),
        ],
        out_specs=pl.BlockSpec((1, ts, c), lambda bi, i: (bi, i, 0)),
        out_shape=jax.ShapeDtypeStruct((batch, seq, c), BF16),
        scratch_shapes=[
            pltpu.VMEM(((ts + 2 * HALO) * sl, LANES), F32),
            pltpu.VMEM((ts * sl, LANES), F32),
        ],
        compiler_params=pltpu.CompilerParams(
            dimension_semantics=("arbitrary", "arbitrary"),
            vmem_limit_bytes=VMEM_LIMIT),
        name="conv_branch",
    )(u_tm, u_tm, u_tm, cw, cb, ln_w, ln_b)


def _merge_kernel(x_ref, og_ref, uc_ref, gr_ref, gc_ref, wro_ref, wco_ref, bco_ref,
                  wout_ref, x2_ref):
    y_ret = jnp.dot(og_ref[...], wro_ref[...], preferred_element_type=F32)
    y_conv = jnp.dot(uc_ref[...], wco_ref[...], preferred_element_type=F32) + bco_ref[...]
    m = gr_ref[...].astype(F32) * y_ret + gc_ref[...].astype(F32) * y_conv
    x2_ref[...] = x_ref[...] + jnp.dot(m.astype(BF16), wout_ref[...],
                                       preferred_element_type=F32)


def _const_spec(shape):
    return pl.BlockSpec(shape, lambda i: (0,) * len(shape), pipeline_mode=pl.Buffered(1))


def _merge(x2d, og2d, uc2d, proj2d, w_ret_o, w_conv_o, b_conv_o, w_out, tm=512):
    m, d = x2d.shape
    vw = og2d.shape[1]
    return pl.pallas_call(
        _merge_kernel,
        grid=(m // tm,),
        in_specs=[
            pl.BlockSpec((tm, d), lambda i: (i, 0)),
            pl.BlockSpec((tm, vw), lambda i: (i, 0)),
            pl.BlockSpec((tm, d), lambda i: (i, 0)),
            pl.BlockSpec((tm, d), lambda i: (i, 6)),
            pl.BlockSpec((tm, d), lambda i: (i, 7)),
            _const_spec((vw, d)),
            _const_spec((d, d)),
            _const_spec((1, d)),
            _const_spec((d, d)),
        ],
        out_specs=pl.BlockSpec((tm, d), lambda i: (i, 0)),
        out_shape=jax.ShapeDtypeStruct((m, d), F32),
        compiler_params=pltpu.CompilerParams(
            dimension_semantics=("arbitrary",),
            vmem_limit_bytes=VMEM_LIMIT),
        name="merge",
    )(x2d, og2d, uc2d, proj2d, proj2d, w_ret_o, w_conv_o, b_conv_o, w_out)


def _mlp_kernel(x_ref, n2_ref, w1_ref, w2_ref, nf_ref, out_ref, *, final_norm, ff_chunk):
    x = x_ref[...]
    h = (_rms_scale(x) * n2_ref[...]).astype(BF16)
    acc = x
    for c0 in range(0, w1_ref.shape[1], ff_chunk):
        a = jnp.dot(h, w1_ref[:, c0:c0 + ff_chunk], preferred_element_type=F32)
        a = jnp.square(jnp.maximum(a, 0.0)).astype(BF16)
        acc = acc + jnp.dot(a, w2_ref[c0:c0 + ff_chunk, :], preferred_element_type=F32)
    if final_norm:
        acc = _rms_scale(acc) * nf_ref[...]
    out_ref[...] = acc


def _mlp(x2d, n2, w1, w2, nf, final_norm, tm=512):
    m, d = x2d.shape
    dff = w1.shape[1]
    return pl.pallas_call(
        functools.partial(_mlp_kernel, final_norm=final_norm, ff_chunk=1024),
        grid=(m // tm,),
        in_specs=[
            pl.BlockSpec((tm, d), lambda i: (i, 0)),
            _const_spec((1, d)),
            _const_spec((d, dff)),
            _const_spec((dff, d)),
            _const_spec((1, d)),
        ],
        out_specs=pl.BlockSpec((tm, d), lambda i: (i, 0)),
        out_shape=jax.ShapeDtypeStruct((m, d), F32),
        compiler_params=pltpu.CompilerParams(
            dimension_semantics=("arbitrary",),
            vmem_limit_bytes=VMEM_LIMIT),
        name="mlp",
    )(x2d, n2, w1, w2, nf)


def kernel(x, norm1_w, w_in, ret_decay_raw, ret_gn_w, w_ret_o, b_glu, conv_w, conv_b, conv_ln_w,
           conv_ln_b, w_conv_o, b_conv_o, w_out, norm2_w, w_mlp1, w_mlp2, norm_f_w):
    b, s, d = x.shape
    depth = w_in.shape[0]

    inv_freq = ROPE_BASE ** (-jnp.arange(0, RET_QK_DIM, 2, dtype=F32) / RET_QK_DIM)
    ang = jnp.arange(s, dtype=F32)[:, None] * inv_freq[None, :]
    cos2 = jnp.concatenate([jnp.cos(ang), jnp.cos(ang)], axis=-1)
    sin2 = jnp.concatenate([-jnp.sin(ang), jnp.sin(ang)], axis=-1)

    x2d = x.reshape(b * s, d)
    for l in range(depth):
        proj, u = _in_proj(x2d, norm1_w[l][None, :], w_in[l].astype(BF16), cos2, sin2,
                           b_glu[l][None, :], ret_gn_w[l][None, :], s)
        raw_b = jnp.broadcast_to(ret_decay_raw[l].T[:, :, None], (RET_HEADS, 2, RET_V_DIM))
        og = _retention(proj.reshape(b, s, -1), raw_b)
        uc = _conv_branch(u, conv_w[l], conv_b[l], conv_ln_w[l][None, :],
                          conv_ln_b[l][None, :], b, s)
        x2d = _merge(x2d, og.reshape(b * s, -1), uc.reshape(b * s, -1), proj,
                     w_ret_o[l].astype(BF16), w_conv_o[l].astype(BF16), b_conv_o[l][None, :],
                     w_out[l].astype(BF16))
        x2d = _mlp(x2d, norm2_w[l][None, :], w_mlp1[l].astype(BF16), w_mlp2[l].astype(BF16),
                   norm_f_w[None, :], final_norm=(l == depth - 1))
    return x2d.reshape(b, s, d)
```

```python
import functools

import jax
import jax.numpy as jnp
from jax import lax
from jax.experimental import pallas as pl
from jax.experimental.pallas import tpu as pltpu

F32 = jnp.float32
BF16 = jnp.bfloat16

RET_HEADS = 8
RET_QK_DIM = 128
RET_V_DIM = 256
CONV_WIDTH = 31
CONV_HALF = CONV_WIDTH // 2
ROPE_BASE = 10000.0
EPS = 1e-6
GN_EPS = 1e-5

LANES = 128
SUBLANES = 8
HALO = 16
RET_CHUNK = 256
VMEM_LIMIT = 56 * 1024 * 1024


def _sigmoid(x):
    return 1.0 / (1.0 + jnp.exp(-x))


def _rms_scale(x):
    return x * lax.rsqrt(jnp.mean(x * x, axis=-1, keepdims=True) + EPS)


def _const_spec(shape):
    return pl.BlockSpec(shape, lambda i: (0,) * len(shape), pipeline_mode=pl.Buffered(1))


def _in_proj_kernel(x_ref, n1_ref, w_ref, cos_ref, sin_ref, bglu_ref, gnw_ref,
                    q_ref, k_ref, v_ref, sg_ref, gates_ref, u_ref, *, k_scale, nw, row_splits):
    tm = x_ref.shape[0]
    rows = tm // row_splits
    hv = nw // RET_V_DIM
    for r0 in range(0, tm, rows):
        rs = slice(r0, r0 + rows)
        h = (_rms_scale(x_ref[rs, :]) * n1_ref[...]).astype(BF16)

        def dots(group, h=h):
            lo = 2 * group * nw
            a = jnp.dot(h, w_ref[:, lo:lo + nw], preferred_element_type=F32)
            b = jnp.dot(h, w_ref[:, lo + nw:lo + 2 * nw], preferred_element_type=F32)
            return a, b

        a, b = dots(0)
        c = cos_ref[rs, :]
        s = sin_ref[rs, :]
        for r, cc, ss, dst in ((a, c, s, q_ref), (b, c * k_scale, s * k_scale, k_ref)):
            for hh in range(nw // RET_QK_DIM):
                t = r[:, hh * RET_QK_DIM:(hh + 1) * RET_QK_DIM]
                rot = pltpu.roll(t, RET_QK_DIM // 2, axis=1)
                dst[0, hh, rs, :] = (t * cc + rot * ss).astype(BF16)

        for half, r in enumerate(dots(2)):
            y = (r * _sigmoid(r) * gnw_ref[:, half * nw:(half + 1) * nw]).astype(BF16)
            for hh in range(hv):
                sg_ref[0, half * hv + hh, rs, :] = y[:, hh * RET_V_DIM:(hh + 1) * RET_V_DIM]

        a, b = dots(3)
        u = (a + bglu_ref[:, :nw]) * _sigmoid(b + bglu_ref[:, nw:])
        for g in range(nw // LANES):
            u_ref[pl.ds(r0 * SUBLANES + g, rows, stride=SUBLANES), :] = (
                u[:, g * LANES:(g + 1) * LANES])

        a, b = dots(4)
        gates_ref[rs, :nw] = _sigmoid(a).astype(BF16)
        gates_ref[rs, nw:] = _sigmoid(b).astype(BF16)

        for half, r in enumerate(dots(1)):
            y = r.astype(BF16)
            for hh in range(hv):
                v_ref[0, half * hv + hh, rs, :] = y[:, hh * RET_V_DIM:(hh + 1) * RET_V_DIM]


def _in_proj(x2d, n1, w_in, cos2, sin2, b_glu, gnw, batch, seq, tm=512):
    m, d = x2d.shape
    sblocks = seq // tm
    nw = 1024
    h = RET_HEADS
    assert nw // LANES == SUBLANES and w_in.shape[1] == 10 * nw
    assert nw == h * RET_QK_DIM and 2 * nw == h * RET_V_DIM

    def head_spec(width):
        return pl.BlockSpec((1, h, tm, width), lambda i: (i // sblocks, 0, i % sblocks, 0))

    return pl.pallas_call(
        functools.partial(_in_proj_kernel, k_scale=RET_QK_DIM ** -0.5, nw=nw, row_splits=2),
        grid=(m // tm,),
        in_specs=[
            pl.BlockSpec((tm, d), lambda i: (i, 0)),
            _const_spec((1, d)),
            _const_spec((d, 10 * nw)),
            pl.BlockSpec((tm, RET_QK_DIM), lambda i: (i % sblocks, 0)),
            pl.BlockSpec((tm, RET_QK_DIM), lambda i: (i % sblocks, 0)),
            _const_spec((1, 2 * nw)),
            _const_spec((1, 2 * nw)),
        ],
        out_specs=[
            head_spec(RET_QK_DIM),
            head_spec(RET_QK_DIM),
            head_spec(RET_V_DIM),
            head_spec(RET_V_DIM),
            pl.BlockSpec((tm, 2 * nw), lambda i: (i, 0)),
            pl.BlockSpec((tm * SUBLANES, LANES), lambda i: (i, 0)),
        ],
        out_shape=[
            jax.ShapeDtypeStruct((batch, h, seq, RET_QK_DIM), BF16),
            jax.ShapeDtypeStruct((batch, h, seq, RET_QK_DIM), BF16),
            jax.ShapeDtypeStruct((batch, h, seq, RET_V_DIM), BF16),
            jax.ShapeDtypeStruct((batch, h, seq, RET_V_DIM), BF16),
            jax.ShapeDtypeStruct((m, 2 * nw), BF16),
            jax.ShapeDtypeStruct((m * SUBLANES, LANES), F32),
        ],
        compiler_params=pltpu.CompilerParams(
            dimension_semantics=("arbitrary",),
            vmem_limit_bytes=VMEM_LIMIT),
        name="in_proj",
    )(x2d, n1, w_in, cos2, sin2, b_glu, gnw)


def _ret_kernel(raw_ref, q_ref, k_ref, v_ref, sg_ref, o_ref,
                rst_ref, d_ref, zeta_ref, xi_ref, *, unroll):
    c = RET_CHUNK
    dk = RET_QK_DIM
    nchunks = q_ref.shape[2] // c
    lg = -jnp.exp(raw_ref[0])
    lgf = lg[0:1, :]
    lgb = lg[1:2, :]

    row = lax.broadcasted_iota(jnp.int32, (c, c), 0)
    col = lax.broadcasted_iota(jnp.int32, (c, c), 1)
    dpos = jnp.maximum(row - col, 0).astype(F32)
    dneg = jnp.maximum(col - row, 0).astype(F32)
    d_ref[...] = jnp.where(row >= col, jnp.exp(lgf * dpos), jnp.exp(lgb * dneg))

    pos = lax.broadcasted_iota(jnp.int32, (c, dk), 0).astype(F32)
    lgf_k = lgf[:, :dk]
    lgb_k = lgb[:, :dk]
    zeta_ref[0] = jnp.exp(lgf_k * (c - 1.0 - pos)).astype(BF16)
    zeta_ref[1] = jnp.exp(lgb_k * pos).astype(BF16)
    xi_ref[:, :dk] = jnp.exp(lgf_k * (pos + 1.0)).astype(BF16)
    xi_ref[:, dk:] = jnp.exp(lgb_k * (c - pos)).astype(BF16)
    g_f = jnp.exp(lgf * c)
    g_b = jnp.exp(lgb * c)

    tdims = (((0,), (0,)), ((), ()))

    def state_body(t, carry):
        rf, rb = carry
        nf = t
        nb = nchunks - 1 - t
        rst_ref[nf, 0:dk, :] = rf.astype(BF16)
        rst_ref[nb, dk:, :] = rb.astype(BF16)
        sf = pl.ds(pl.multiple_of(nf * c, c), c)
        sb = pl.ds(pl.multiple_of(nb * c, c), c)
        kzf = k_ref[0, 0, sf, :] * zeta_ref[0]
        kzb = k_ref[0, 0, sb, :] * zeta_ref[1]
        uf = lax.dot_general(kzf, v_ref[0, 0, sf, :], tdims, preferred_element_type=F32)
        ub = lax.dot_general(kzb, v_ref[0, 0, sb, :], tdims, preferred_element_type=F32)
        return g_f * rf + uf, g_b * rb + ub

    zero = jnp.zeros((dk, RET_V_DIM), F32)
    lax.fori_loop(0, nchunks, state_body, (zero, zero), unroll=unroll)

    def out_body(n, _):
        sl = pl.ds(pl.multiple_of(n * c, c), c)
        qc = q_ref[0, 0, sl, :]
        s = lax.dot_general(qc, k_ref[0, 0, sl, :], (((1,), (1,)), ((), ())),
                            preferred_element_type=F32)
        p = (s * d_ref[...]).astype(BF16)
        qx = jnp.concatenate([qc, qc], axis=1) * xi_ref[...]
        o = (jnp.dot(p, v_ref[0, 0, sl, :], preferred_element_type=F32)
             + jnp.dot(qx, rst_ref[n], preferred_element_type=F32))
        mu = jnp.mean(o, axis=-1, keepdims=True)
        oc = o - mu
        var = jnp.mean(oc * oc, axis=-1, keepdims=True)
        o_ref[0, 0, sl, :] = (oc * lax.rsqrt(var + GN_EPS)).astype(BF16) * sg_ref[0, 0, sl, :]
        return 0

    lax.fori_loop(0, nchunks, out_body, 0, unroll=unroll)


def _retention(q4, k4, v4, sg4, raw_b):
    b, h, s, _ = q4.shape
    nchunks = s // RET_CHUNK

    def head_spec(width):
        return pl.BlockSpec((1, 1, s, width), lambda bi, hi: (bi, hi, 0, 0))

    return pl.pallas_call(
        functools.partial(_ret_kernel, unroll=8),
        grid=(b, h),
        in_specs=[
            pl.BlockSpec((1, 2, RET_V_DIM), lambda bi, hi: (hi, 0, 0)),
            head_spec(RET_QK_DIM),
            head_spec(RET_QK_DIM),
            head_spec(RET_V_DIM),
            head_spec(RET_V_DIM),
        ],
        out_specs=head_spec(RET_V_DIM),
        out_shape=jax.ShapeDtypeStruct((b, h, s, RET_V_DIM), BF16),
        scratch_shapes=[
            pltpu.VMEM((nchunks, 2 * RET_QK_DIM, RET_V_DIM), BF16),
            pltpu.VMEM((RET_CHUNK, RET_CHUNK), F32),
            pltpu.VMEM((2, RET_CHUNK, RET_QK_DIM), BF16),
            pltpu.VMEM((RET_CHUNK, 2 * RET_QK_DIM), BF16),
        ],
        compiler_params=pltpu.CompilerParams(
            dimension_semantics=("arbitrary", "arbitrary"),
            vmem_limit_bytes=VMEM_LIMIT),
        name="retention",
    )(raw_b, q4, k4, v4, sg4)


def _conv_kernel(prev_ref, main_ref, next_ref, cw_ref, cb_ref, lnw_ref, lnb_ref,
                 out_ref, ubuf_ref, acc_ref, *, tblk, ln_rows):
    i = pl.program_id(1)
    last = pl.num_programs(1) - 1
    sl = SUBLANES
    ts = main_ref.shape[0] // sl
    ubuf_ref[0:HALO * sl, :] = jnp.where(i > 0, prev_ref[...], 0.0)
    ubuf_ref[HALO * sl:(HALO + ts) * sl, :] = main_ref[...]
    ubuf_ref[(HALO + ts) * sl:, :] = jnp.where(i < last, next_ref[...], 0.0)

    w = [cw_ref[k * sl:(k + 1) * sl, :] for k in range(CONV_WIDTH)]
    bias = cb_ref[...]

    def conv_body(bi, _):
        t0 = bi * tblk
        accs = [bias] * tblk
        for m in range(tblk + CONV_WIDTH - 1):
            start = pl.multiple_of((t0 + (HALO - CONV_HALF) + m) * sl, sl)
            tok = ubuf_ref[pl.ds(start, sl), :]
            for tt in range(tblk):
                k = m - tt
                if 0 <= k < CONV_WIDTH:
                    accs[tt] = accs[tt] + tok * w[k]
        for tt in range(tblk):
            acc_ref[pl.ds(pl.multiple_of((t0 + tt) * sl, sl), sl), :] = accs[tt]
        return 0

    lax.fori_loop(0, ts // tblk, conv_body, 0)

    lnw = lnw_ref[...]
    lnb = lnb_ref[...]

    def ln_body(ri, _):
        r0 = pl.multiple_of(ri * ln_rows, ln_rows)
        a = jnp.concatenate(
            [acc_ref[pl.ds(r0 * sl + g, ln_rows, stride=sl), :] for g in range(sl)], axis=1)
        mu = jnp.mean(a, axis=-1, keepdims=True)
        ac = a - mu
        var = jnp.mean(ac * ac, axis=-1, keepdims=True)
        y = ac * lax.rsqrt(var + GN_EPS) * lnw + lnb
        out_ref[0, pl.ds(r0, ln_rows), :] = (y * _sigmoid(y)).astype(BF16)
        return 0

    lax.fori_loop(0, ts // ln_rows, ln_body, 0, unroll=2)


def _conv_branch(u_tm, conv_w, conv_b, ln_w, ln_b, batch, seq, ts=512):
    c = conv_w.shape[1]
    sl = SUBLANES
    nblk = seq // ts
    hb = ts // HALO
    nhalo = batch * seq // HALO
    cw = conv_w.reshape(CONV_WIDTH * sl, LANES)
    cb = conv_b.reshape(sl, LANES)
    return pl.pallas_call(
        functools.partial(_conv_kernel, tblk=16, ln_rows=64),
        grid=(batch, nblk),
        in_specs=[
            pl.BlockSpec((HALO * sl, LANES),
                         lambda bi, i: (jnp.maximum((bi * nblk + i) * hb - 1, 0), 0)),
            pl.BlockSpec((ts * sl, LANES), lambda bi, i: (bi * nblk + i, 0)),
            pl.BlockSpec((HALO * sl, LANES),
                         lambda bi, i: (jnp.minimum((bi * nblk + i + 1) * hb, nhalo - 1), 0)),
            pl.BlockSpec((CONV_WIDTH * sl, LANES), lambda bi, i: (0, 0)),
            pl.BlockSpec((sl, LANES), lambda bi, i: (0, 0)),
            pl.BlockSpec((1, c), lambda bi, i: (0, 0)),
            pl.BlockSpec((1, c), lambda bi, i: (0, 0)),
        ],
        out_specs=pl.BlockSpec((1, ts, c), lambda bi, i: (bi, i, 0)),
        out_shape=jax.ShapeDtypeStruct((batch, seq, c), BF16),
        scratch_shapes=[
            pltpu.VMEM(((ts + 2 * HALO) * sl, LANES), F32),
            pltpu.VMEM((ts * sl, LANES), F32),
        ],
        compiler_params=pltpu.CompilerParams(
            dimension_semantics=("arbitrary", "arbitrary"),
            vmem_limit_bytes=VMEM_LIMIT),
        name="conv_branch",
    )(u_tm, u_tm, u_tm, cw, cb, ln_w, ln_b)


def _merge_kernel(x_ref, og_ref, uc_ref, gr_ref, gc_ref, wro_ref, wco_ref, bco_ref,
                  wout_ref, x2_ref):
    og = jnp.concatenate([og_ref[0, hh] for hh in range(og_ref.shape[1])], axis=1)
    y_ret = jnp.dot(og, wro_ref[...], preferred_element_type=F32)
    y_conv = jnp.dot(uc_ref[...], wco_ref[...], preferred_element_type=F32) + bco_ref[...]
    m = gr_ref[...].astype(F32) * y_ret + gc_ref[...].astype(F32) * y_conv
    x2_ref[...] = x_ref[...] + jnp.dot(m.astype(BF16), wout_ref[...],
                                       preferred_element_type=F32)


def _merge(x2d, og4, uc2d, gates, w_ret_o, w_conv_o, b_conv_o, w_out, tm=512):
    m, d = x2d.shape
    _, h, seq, dv = og4.shape
    vw = h * dv
    sblocks = seq // tm
    return pl.pallas_call(
        _merge_kernel,
        grid=(m // tm,),
        in_specs=[
            pl.BlockSpec((tm, d), lambda i: (i, 0)),
            pl.BlockSpec((1, h, tm, dv), lambda i: (i // sblocks, 0, i % sblocks, 0)),
            pl.BlockSpec((tm, d), lambda i: (i, 0)),
            pl.BlockSpec((tm, d), lambda i: (i, 0)),
            pl.BlockSpec((tm, d), lambda i: (i, 1)),
            _const_spec((vw, d)),
            _const_spec((d, d)),
            _const_spec((1, d)),
            _const_spec((d, d)),
        ],
        out_specs=pl.BlockSpec((tm, d), lambda i: (i, 0)),
        out_shape=jax.ShapeDtypeStruct((m, d), F32),
        compiler_params=pltpu.CompilerParams(
            dimension_semantics=("arbitrary",),
            vmem_limit_bytes=VMEM_LIMIT),
        name="merge",
    )(x2d, og4, uc2d, gates, gates, w_ret_o, w_conv_o, b_conv_o, w_out)


def _mlp_kernel(x_ref, n2_ref, w1_ref, w2_ref, nf_ref, out_ref, *, final_norm, ff_chunk):
    x = x_ref[...]
    h = (_rms_scale(x) * n2_ref[...]).astype(BF16)
    acc = x
    for c0 in range(0, w1_ref.shape[1], ff_chunk):
        a = jnp.dot(h, w1_ref[:, c0:c0 + ff_chunk], preferred_element_type=F32)
        a = jnp.square(jnp.maximum(a, 0.0)).astype(BF16)
        acc = acc + jnp.dot(a, w2_ref[c0:c0 + ff_chunk, :], preferred_element_type=F32)
    if final_norm:
        acc = _rms_scale(acc) * nf_ref[...]
    out_ref[...] = acc


def _mlp(x2d, n2, w1, w2, nf, final_norm, tm=512):
    m, d = x2d.shape
    dff = w1.shape[1]
    return pl.pallas_call(
        functools.partial(_mlp_kernel, final_norm=final_norm, ff_chunk=1024),
        grid=(m // tm,),
        in_specs=[
            pl.BlockSpec((tm, d), lambda i: (i, 0)),
            _const_spec((1, d)),
            _const_spec((d, dff)),
            _const_spec((dff, d)),
            _const_spec((1, d)),
        ],
        out_specs=pl.BlockSpec((tm, d), lambda i: (i, 0)),
        out_shape=jax.ShapeDtypeStruct((m, d), F32),
        compiler_params=pltpu.CompilerParams(
            dimension_semantics=("arbitrary",),
            vmem_limit_bytes=VMEM_LIMIT),
        name="mlp",
    )(x2d, n2, w1, w2, nf)


def kernel(x, norm1_w, w_in, ret_decay_raw, ret_gn_w, w_ret_o, b_glu, conv_w, conv_b, conv_ln_w,
           conv_ln_b, w_conv_o, b_conv_o, w_out, norm2_w, w_mlp1, w_mlp2, norm_f_w):
    b, s, d = x.shape
    depth = w_in.shape[0]

    inv_freq = ROPE_BASE ** (-jnp.arange(0, RET_QK_DIM, 2, dtype=F32) / RET_QK_DIM)
    ang = jnp.arange(s, dtype=F32)[:, None] * inv_freq[None, :]
    cos2 = jnp.concatenate([jnp.cos(ang), jnp.cos(ang)], axis=-1)
    sin2 = jnp.concatenate([-jnp.sin(ang), jnp.sin(ang)], axis=-1)

    x2d = x.reshape(b * s, d)
    for l in range(depth):
        q4, k4, v4, sg4, gates, u = _in_proj(
            x2d, norm1_w[l][None, :], w_in[l].astype(BF16), cos2, sin2, b_glu[l][None, :],
            ret_gn_w[l][None, :], b, s)
        raw_b = jnp.broadcast_to(ret_decay_raw[l].T[:, :, None], (RET_HEADS, 2, RET_V_DIM))
        og4 = _retention(q4, k4, v4, sg4, raw_b)
        uc = _conv_branch(u, conv_w[l], conv_b[l], conv_ln_w[l][None, :],
                          conv_ln_b[l][None, :], b, s)
        x2d = _merge(x2d, og4, uc.reshape(b * s, -1), gates,
                     w_ret_o[l].astype(BF16), w_conv_o[l].astype(BF16), b_conv_o[l][None, :],
                     w_out[l].astype(BF16))
        x2d = _mlp(x2d, norm2_w[l][None, :], w_mlp1[l].astype(BF16), w_mlp2[l].astype(BF16),
                   norm_f_w[None, :], final_norm=(l == depth - 1))
    return x2d.reshape(b, s, d)
```

```python
import functools

import jax
import jax.numpy as jnp
from jax import lax
from jax.experimental import pallas as pl
from jax.experimental.pallas import tpu as pltpu

F32 = jnp.float32
BF16 = jnp.bfloat16

RET_HEADS = 8
RET_QK_DIM = 128
RET_V_DIM = 256
CONV_WIDTH = 31
CONV_HALF = CONV_WIDTH // 2
ROPE_BASE = 10000.0
EPS = 1e-6
GN_EPS = 1e-5

LANES = 128
SUBLANES = 8
HALO = 16
RET_CHUNK = 256
VMEM_LIMIT = 56 * 1024 * 1024


def _sigmoid(x):
    return 1.0 / (1.0 + jnp.exp(-x))


def _rms_scale(x):
    return x * lax.rsqrt(jnp.mean(x * x, axis=-1, keepdims=True) + EPS)


def _const_spec(shape):
    return pl.BlockSpec(shape, lambda i: (0,) * len(shape), pipeline_mode=pl.Buffered(1))


def _in_proj_kernel(x_ref, n1_ref, w_ref, cos_ref, sin_ref, bglu_ref, gnw_ref,
                    q_ref, k_ref, v_ref, sg_ref, gates_ref, u_ref, *, k_scale, nw, row_splits):
    tm = x_ref.shape[0]
    rows = tm // row_splits
    hv = nw // RET_V_DIM
    for r0 in range(0, tm, rows):
        rs = slice(r0, r0 + rows)
        h = (_rms_scale(x_ref[rs, :]) * n1_ref[...]).astype(BF16)

        def dots(group, h=h):
            lo = 2 * group * nw
            a = jnp.dot(h, w_ref[:, lo:lo + nw], preferred_element_type=F32)
            b = jnp.dot(h, w_ref[:, lo + nw:lo + 2 * nw], preferred_element_type=F32)
            return a, b

        a, b = dots(0)
        c = cos_ref[rs, :]
        s = sin_ref[rs, :]
        for r, cc, ss, dst in ((a, c, s, q_ref), (b, c * k_scale, s * k_scale, k_ref)):
            for hh in range(nw // RET_QK_DIM):
                t = r[:, hh * RET_QK_DIM:(hh + 1) * RET_QK_DIM]
                rot = pltpu.roll(t, RET_QK_DIM // 2, axis=1)
                dst[0, hh, rs, :] = (t * cc + rot * ss).astype(BF16)

        for half, r in enumerate(dots(2)):
            y = (r * _sigmoid(r) * gnw_ref[:, half * nw:(half + 1) * nw]).astype(BF16)
            for hh in range(hv):
                sg_ref[0, half * hv + hh, rs, :] = y[:, hh * RET_V_DIM:(hh + 1) * RET_V_DIM]

        a, b = dots(3)
        u = (a + bglu_ref[:, :nw]) * _sigmoid(b + bglu_ref[:, nw:])
        for g in range(nw // LANES):
            u_ref[pl.ds(r0 * SUBLANES + g, rows, stride=SUBLANES), :] = (
                u[:, g * LANES:(g + 1) * LANES])

        a, b = dots(4)
        gates_ref[rs, :nw] = _sigmoid(a).astype(BF16)
        gates_ref[rs, nw:] = _sigmoid(b).astype(BF16)

        for half, r in enumerate(dots(1)):
            y = r.astype(BF16)
            for hh in range(hv):
                v_ref[0, half * hv + hh, rs, :] = y[:, hh * RET_V_DIM:(hh + 1) * RET_V_DIM]


def _in_proj(x2d, n1, w_in, cos2, sin2, b_glu, gnw, batch, seq, tm=512):
    m, d = x2d.shape
    sblocks = seq // tm
    nw = 1024
    h = RET_HEADS
    assert nw // LANES == SUBLANES and w_in.shape[1] == 10 * nw
    assert nw == h * RET_QK_DIM and 2 * nw == h * RET_V_DIM

    def head_spec(width):
        return pl.BlockSpec((1, h, tm, width), lambda i: (i // sblocks, 0, i % sblocks, 0))

    return pl.pallas_call(
        functools.partial(_in_proj_kernel, k_scale=RET_QK_DIM ** -0.5, nw=nw, row_splits=2),
        grid=(m // tm,),
        in_specs=[
            pl.BlockSpec((tm, d), lambda i: (i, 0)),
            _const_spec((1, d)),
            _const_spec((d, 10 * nw)),
            pl.BlockSpec((tm, RET_QK_DIM), lambda i: (i % sblocks, 0)),
            pl.BlockSpec((tm, RET_QK_DIM), lambda i: (i % sblocks, 0)),
            _const_spec((1, 2 * nw)),
            _const_spec((1, 2 * nw)),
        ],
        out_specs=[
            head_spec(RET_QK_DIM),
            head_spec(RET_QK_DIM),
            head_spec(RET_V_DIM),
            head_spec(RET_V_DIM),
            pl.BlockSpec((tm, 2 * nw), lambda i: (i, 0)),
            pl.BlockSpec((tm * SUBLANES, LANES), lambda i: (i, 0)),
        ],
        out_shape=[
            jax.ShapeDtypeStruct((batch, h, seq, RET_QK_DIM), BF16),
            jax.ShapeDtypeStruct((batch, h, seq, RET_QK_DIM), BF16),
            jax.ShapeDtypeStruct((batch, h, seq, RET_V_DIM), BF16),
            jax.ShapeDtypeStruct((batch, h, seq, RET_V_DIM), BF16),
            jax.ShapeDtypeStruct((m, 2 * nw), BF16),
            jax.ShapeDtypeStruct((m * SUBLANES, LANES), F32),
        ],
        compiler_params=pltpu.CompilerParams(
            dimension_semantics=("arbitrary",),
            vmem_limit_bytes=VMEM_LIMIT),
        name="in_proj",
    )(x2d, n1, w_in, cos2, sin2, b_glu, gnw)


def _ret_kernel(raw_ref, q_ref, k_ref, v_ref, sg_ref, o_ref,
                rst_ref, d_ref, zeta_ref, xi_ref, *, unroll):
    c = RET_CHUNK
    dk = RET_QK_DIM
    nchunks = q_ref.shape[2] // c
    lg = -jnp.exp(raw_ref[0])
    lgf = lg[0:1, :]
    lgb = lg[1:2, :]

    row = lax.broadcasted_iota(jnp.int32, (c, c), 0)
    col = lax.broadcasted_iota(jnp.int32, (c, c), 1)
    dpos = jnp.maximum(row - col, 0).astype(F32)
    dneg = jnp.maximum(col - row, 0).astype(F32)
    d_ref[...] = jnp.where(row >= col, jnp.exp(lgf * dpos), jnp.exp(lgb * dneg))

    pos = lax.broadcasted_iota(jnp.int32, (c, dk), 0).astype(F32)
    lgf_k = lgf[:, :dk]
    lgb_k = lgb[:, :dk]
    zeta_ref[0] = jnp.exp(lgf_k * (c - 1.0 - pos)).astype(BF16)
    zeta_ref[1] = jnp.exp(lgb_k * pos).astype(BF16)
    xi_ref[:, :dk] = jnp.exp(lgf_k * (pos + 1.0)).astype(BF16)
    xi_ref[:, dk:] = jnp.exp(lgb_k * (c - pos)).astype(BF16)
    g_f = jnp.exp(lgf * c)
    g_b = jnp.exp(lgb * c)

    tdims = (((0,), (0,)), ((), ()))

    def state_body(t, carry):
        rf, rb = carry
        nf = t
        nb = nchunks - 1 - t
        rst_ref[nf, 0:dk, :] = rf.astype(BF16)
        rst_ref[nb, dk:, :] = rb.astype(BF16)
        sf = pl.ds(pl.multiple_of(nf * c, c), c)
        sb = pl.ds(pl.multiple_of(nb * c, c), c)
        kzf = k_ref[0, 0, sf, :] * zeta_ref[0]
        kzb = k_ref[0, 0, sb, :] * zeta_ref[1]
        uf = lax.dot_general(kzf, v_ref[0, 0, sf, :], tdims, preferred_element_type=F32)
        ub = lax.dot_general(kzb, v_ref[0, 0, sb, :], tdims, preferred_element_type=F32)
        return g_f * rf + uf, g_b * rb + ub

    zero = jnp.zeros((dk, RET_V_DIM), F32)
    lax.fori_loop(0, nchunks, state_body, (zero, zero), unroll=unroll)

    def out_body(n, _):
        sl = pl.ds(pl.multiple_of(n * c, c), c)
        qc = q_ref[0, 0, sl, :]
        s = lax.dot_general(qc, k_ref[0, 0, sl, :], (((1,), (1,)), ((), ())),
                            preferred_element_type=F32)
        p = (s * d_ref[...]).astype(BF16)
        qx = jnp.concatenate([qc, qc], axis=1) * xi_ref[...]
        o = (jnp.dot(p, v_ref[0, 0, sl, :], preferred_element_type=F32)
             + jnp.dot(qx, rst_ref[n], preferred_element_type=F32))
        mu = jnp.mean(o, axis=-1, keepdims=True)
        oc = o - mu
        var = jnp.mean(oc * oc, axis=-1, keepdims=True)
        o_ref[0, 0, sl, :] = (oc * lax.rsqrt(var + GN_EPS)).astype(BF16) * sg_ref[0, 0, sl, :]
        return 0

    lax.fori_loop(0, nchunks, out_body, 0, unroll=unroll)


def _retention(q4, k4, v4, sg4, raw_b):
    b, h, s, _ = q4.shape
    nchunks = s // RET_CHUNK

    def head_spec(width):
        return pl.BlockSpec((1, 1, s, width), lambda bi, hi: (bi, hi, 0, 0))

    return pl.pallas_call(
        functools.partial(_ret_kernel, unroll=8),
        grid=(b, h),
        in_specs=[
            pl.BlockSpec((1, 2, RET_V_DIM), lambda bi, hi: (hi, 0, 0)),
            head_spec(RET_QK_DIM),
            head_spec(RET_QK_DIM),
            head_spec(RET_V_DIM),
            head_spec(RET_V_DIM),
        ],
        out_specs=head_spec(RET_V_DIM),
        out_shape=jax.ShapeDtypeStruct((b, h, s, RET_V_DIM), BF16),
        scratch_shapes=[
            pltpu.VMEM((nchunks, 2 * RET_QK_DIM, RET_V_DIM), BF16),
            pltpu.VMEM((RET_CHUNK, RET_CHUNK), F32),
            pltpu.VMEM((2, RET_CHUNK, RET_QK_DIM), BF16),
            pltpu.VMEM((RET_CHUNK, 2 * RET_QK_DIM), BF16),
        ],
        compiler_params=pltpu.CompilerParams(
            dimension_semantics=("arbitrary", "arbitrary"),
            vmem_limit_bytes=VMEM_LIMIT),
        name="retention",
    )(raw_b, q4, k4, v4, sg4)


def _conv_kernel(prev_ref, main_ref, next_ref, cw_ref, cb_ref, out_ref, ubuf_ref, *, tblk):
    i = pl.program_id(1)
    last = pl.num_programs(1) - 1
    sl = SUBLANES
    ts = main_ref.shape[0] // sl
    ubuf_ref[0:HALO * sl, :] = jnp.where(i > 0, prev_ref[...], 0.0)
    ubuf_ref[HALO * sl:(HALO + ts) * sl, :] = main_ref[...]
    ubuf_ref[(HALO + ts) * sl:, :] = jnp.where(i < last, next_ref[...], 0.0)

    w = [cw_ref[k * sl:(k + 1) * sl, :] for k in range(CONV_WIDTH)]
    bias = cb_ref[...]

    def conv_body(bi, _):
        t0 = bi * tblk
        accs = [bias] * tblk
        for m in range(tblk + CONV_WIDTH - 1):
            start = pl.multiple_of((t0 + (HALO - CONV_HALF) + m) * sl, sl)
            tok = ubuf_ref[pl.ds(start, sl), :]
            for tt in range(tblk):
                k = m - tt
                if 0 <= k < CONV_WIDTH:
                    accs[tt] = accs[tt] + tok * w[k]
        for tt in range(tblk):
            out_ref[pl.ds(pl.multiple_of((t0 + tt) * sl, sl), sl), :] = accs[tt]
        return 0

    lax.fori_loop(0, ts // tblk, conv_body, 0)


def _conv_sums(u_tm, conv_w, conv_b, batch, seq, ts=512):
    sl = SUBLANES
    nblk = seq // ts
    hb = ts // HALO
    nhalo = batch * seq // HALO
    cw = conv_w.reshape(CONV_WIDTH * sl, LANES)
    cb = conv_b.reshape(sl, LANES)
    return pl.pallas_call(
        functools.partial(_conv_kernel, tblk=16),
        grid=(batch, nblk),
        in_specs=[
            pl.BlockSpec((HALO * sl, LANES),
                         lambda bi, i: (jnp.maximum((bi * nblk + i) * hb - 1, 0), 0)),
            pl.BlockSpec((ts * sl, LANES), lambda bi, i: (bi * nblk + i, 0)),
            pl.BlockSpec((HALO * sl, LANES),
                         lambda bi, i: (jnp.minimum((bi * nblk + i + 1) * hb, nhalo - 1), 0)),
            pl.BlockSpec((CONV_WIDTH * sl, LANES), lambda bi, i: (0, 0)),
            pl.BlockSpec((sl, LANES), lambda bi, i: (0, 0)),
        ],
        out_specs=pl.BlockSpec((ts * sl, LANES), lambda bi, i: (bi * nblk + i, 0)),
        out_shape=jax.ShapeDtypeStruct(u_tm.shape, F32),
        scratch_shapes=[pltpu.VMEM(((ts + 2 * HALO) * sl, LANES), F32)],
        compiler_params=pltpu.CompilerParams(
            dimension_semantics=("arbitrary", "arbitrary"),
            vmem_limit_bytes=VMEM_LIMIT),
        name="conv_sums",
    )(u_tm, u_tm, u_tm, cw, cb)


def _merge_kernel(x_ref, og_ref, cs_ref, gr_ref, gc_ref, lnw_ref, lnb_ref, wro_ref, wco_ref,
                  bco_ref, wout_ref, x2_ref, *, row_splits):
    tm = x_ref.shape[0]
    rows = tm // row_splits
    sl = SUBLANES
    for r0 in range(0, tm, rows):
        rs = slice(r0, r0 + rows)
        og = jnp.concatenate([og_ref[0, hh, rs, :] for hh in range(og_ref.shape[1])], axis=1)
        y_ret = jnp.dot(og, wro_ref[...], preferred_element_type=F32)
        a = jnp.concatenate(
            [cs_ref[pl.ds(r0 * sl + g, rows, stride=sl), :] for g in range(sl)], axis=1)
        mu = jnp.mean(a, axis=-1, keepdims=True)
        ac = a - mu
        var = jnp.mean(ac * ac, axis=-1, keepdims=True)
        y = ac * lax.rsqrt(var + GN_EPS) * lnw_ref[...] + lnb_ref[...]
        uc = (y * _sigmoid(y)).astype(BF16)
        y_conv = jnp.dot(uc, wco_ref[...], preferred_element_type=F32) + bco_ref[...]
        m = gr_ref[rs, :].astype(F32) * y_ret + gc_ref[rs, :].astype(F32) * y_conv
        x2_ref[rs, :] = x_ref[rs, :] + jnp.dot(m.astype(BF16), wout_ref[...],
                                               preferred_element_type=F32)


def _merge(x2d, og4, csum_tm, gates, ln_w, ln_b, w_ret_o, w_conv_o, b_conv_o, w_out, tm=512):
    m, d = x2d.shape
    _, h, seq, dv = og4.shape
    vw = h * dv
    sblocks = seq // tm
    return pl.pallas_call(
        functools.partial(_merge_kernel, row_splits=2),
        grid=(m // tm,),
        in_specs=[
            pl.BlockSpec((tm, d), lambda i: (i, 0)),
            pl.BlockSpec((1, h, tm, dv), lambda i: (i // sblocks, 0, i % sblocks, 0)),
            pl.BlockSpec((tm * SUBLANES, LANES), lambda i: (i, 0)),
            pl.BlockSpec((tm, d), lambda i: (i, 0)),
            pl.BlockSpec((tm, d), lambda i: (i, 1)),
            _const_spec((1, d)),
            _const_spec((1, d)),
            _const_spec((vw, d)),
            _const_spec((d, d)),
            _const_spec((1, d)),
            _const_spec((d, d)),
        ],
        out_specs=pl.BlockSpec((tm, d), lambda i: (i, 0)),
        out_shape=jax.ShapeDtypeStruct((m, d), F32),
        compiler_params=pltpu.CompilerParams(
            dimension_semantics=("arbitrary",),
            vmem_limit_bytes=VMEM_LIMIT),
        name="merge",
    )(x2d, og4, csum_tm, gates, gates, ln_w, ln_b, w_ret_o, w_conv_o, b_conv_o, w_out)


def _mlp_kernel(x_ref, n2_ref, w1_ref, w2_ref, nf_ref, out_ref, *, final_norm, ff_chunk,
                row_splits):
    tm = x_ref.shape[0]
    rows = tm // row_splits
    for r0 in range(0, tm, rows):
        x = x_ref[r0:r0 + rows, :]
        h = (_rms_scale(x) * n2_ref[...]).astype(BF16)
        acc = x
        for c0 in range(0, w1_ref.shape[1], ff_chunk):
            a = jnp.dot(h, w1_ref[:, c0:c0 + ff_chunk], preferred_element_type=F32)
            a = jnp.square(jnp.maximum(a, 0.0)).astype(BF16)
            acc = acc + jnp.dot(a, w2_ref[c0:c0 + ff_chunk, :], preferred_element_type=F32)
        if final_norm:
            acc = _rms_scale(acc) * nf_ref[...]
        out_ref[r0:r0 + rows, :] = acc


def _mlp(x2d, n2, w1, w2, nf, final_norm, tm=1024):
    m, d = x2d.shape
    dff = w1.shape[1]
    return pl.pallas_call(
        functools.partial(_mlp_kernel, final_norm=final_norm, ff_chunk=1024, row_splits=4),
        grid=(m // tm,),
        in_specs=[
            pl.BlockSpec((tm, d), lambda i: (i, 0)),
            _const_spec((1, d)),
            _const_spec((d, dff)),
            _const_spec((dff, d)),
            _const_spec((1, d)),
        ],
        out_specs=pl.BlockSpec((tm, d), lambda i: (i, 0)),
        out_shape=jax.ShapeDtypeStruct((m, d), F32),
        compiler_params=pltpu.CompilerParams(
            dimension_semantics=("arbitrary",),
            vmem_limit_bytes=VMEM_LIMIT),
        name="mlp",
    )(x2d, n2, w1, w2, nf)


def _rotary_tables(seq, block=64):
    inv_freq = ROPE_BASE ** (-jnp.arange(0, RET_QK_DIM, 2, dtype=F32) / RET_QK_DIM)
    ang_hi = (jnp.arange(seq // block, dtype=F32) * block)[:, None] * inv_freq[None, :]
    ang_lo = jnp.arange(block, dtype=F32)[:, None] * inv_freq[None, :]
    ch, sh = jnp.cos(ang_hi)[:, None, :], jnp.sin(ang_hi)[:, None, :]
    cl, sl = jnp.cos(ang_lo)[None, :, :], jnp.sin(ang_lo)[None, :, :]
    cos = (ch * cl - sh * sl).reshape(seq, -1)
    sin = (sh * cl + ch * sl).reshape(seq, -1)
    return (jnp.concatenate([cos, cos], axis=-1), jnp.concatenate([-sin, sin], axis=-1))


def kernel(x, norm1_w, w_in, ret_decay_raw, ret_gn_w, w_ret_o, b_glu, conv_w, conv_b, conv_ln_w,
           conv_ln_b, w_conv_o, b_conv_o, w_out, norm2_w, w_mlp1, w_mlp2, norm_f_w):
    b, s, d = x.shape
    depth = w_in.shape[0]

    cos2, sin2 = _rotary_tables(s)

    x2d = x.reshape(b * s, d)
    for l in range(depth):
        q4, k4, v4, sg4, gates, u = _in_proj(
            x2d, norm1_w[l][None, :], w_in[l].astype(BF16), cos2, sin2, b_glu[l][None, :],
            ret_gn_w[l][None, :], b, s)
        raw_b = jnp.broadcast_to(ret_decay_raw[l].T[:, :, None], (RET_HEADS, 2, RET_V_DIM))
        og4 = _retention(q4, k4, v4, sg4, raw_b)
        csum = _conv_sums(u, conv_w[l], conv_b[l], b, s)
        x2d = _merge(x2d, og4, csum, gates, conv_ln_w[l][None, :], conv_ln_b[l][None, :],
                     w_ret_o[l].astype(BF16), w_conv_o[l].astype(BF16), b_conv_o[l][None, :],
                     w_out[l].astype(BF16))
        x2d = _mlp(x2d, norm2_w[l][None, :], w_mlp1[l].astype(BF16), w_mlp2[l].astype(BF16),
                   norm_f_w[None, :], final_norm=(l == depth - 1))
    return x2d.reshape(b, s, d)
```

```python
import functools

import jax
import jax.numpy as jnp
from jax import lax
from jax.experimental import pallas as pl
from jax.experimental.pallas import tpu as pltpu

F32 = jnp.float32
BF16 = jnp.bfloat16

RET_HEADS = 8
RET_QK_DIM = 128
RET_V_DIM = 256
CONV_WIDTH = 31
CONV_HALF = CONV_WIDTH // 2
ROPE_BASE = 10000.0
EPS = 1e-6
GN_EPS = 1e-5

LANES = 128
SUBLANES = 8
HALO = 16
RET_CHUNK = 256
VMEM_LIMIT = 56 * 1024 * 1024


def _sigmoid(x):
    return 1.0 / (1.0 + jnp.exp(-x))


def _rms_scale(x):
    return x * lax.rsqrt(jnp.mean(x * x, axis=-1, keepdims=True) + EPS)


def _const_spec(shape):
    return pl.BlockSpec(shape, lambda i: (0,) * len(shape), pipeline_mode=pl.Buffered(1))


def _cast_weight_once(w_hbm, w_vmem, stage_ref, sem_ref, *, chunk_rows):
    nchunks = w_hbm.shape[0] // chunk_rows

    def copy(c, slot):
        return pltpu.make_async_copy(
            w_hbm.at[pl.ds(c * chunk_rows, chunk_rows), :], stage_ref.at[slot], sem_ref.at[slot])

    copy(0, 0).start()
    for c in range(nchunks):
        slot = c % 2
        if c + 1 < nchunks:
            copy(c + 1, 1 - slot).start()
        copy(c, slot).wait()
        w_vmem[c * chunk_rows:(c + 1) * chunk_rows, :] = stage_ref[slot].astype(BF16)


def _in_proj_kernel(x_ref, n1_ref, w_hbm, cos_ref, sin_ref, bglu_ref, gnw_ref,
                    q_ref, k_ref, v_ref, sg_ref, gates_ref, u_ref, w_ref, stage_ref, sem_ref,
                    *, k_scale, nw, row_splits):
    @pl.when(pl.program_id(0) == 0)
    def _():
        _cast_weight_once(w_hbm, w_ref, stage_ref, sem_ref, chunk_rows=stage_ref.shape[1])

    tm = x_ref.shape[0]
    rows = tm // row_splits
    hv = nw // RET_V_DIM
    for r0 in range(0, tm, rows):
        rs = slice(r0, r0 + rows)
        h = (_rms_scale(x_ref[rs, :]) * n1_ref[...]).astype(BF16)

        def dots(group, h=h):
            lo = 2 * group * nw
            a = jnp.dot(h, w_ref[:, lo:lo + nw], preferred_element_type=F32)
            b = jnp.dot(h, w_ref[:, lo + nw:lo + 2 * nw], preferred_element_type=F32)
            return a, b

        a, b = dots(0)
        c = cos_ref[rs, :]
        s = sin_ref[rs, :]
        for r, cc, ss, dst in ((a, c, s, q_ref), (b, c * k_scale, s * k_scale, k_ref)):
            for hh in range(nw // RET_QK_DIM):
                t = r[:, hh * RET_QK_DIM:(hh + 1) * RET_QK_DIM]
                rot = pltpu.roll(t, RET_QK_DIM // 2, axis=1)
                dst[0, hh, rs, :] = (t * cc + rot * ss).astype(BF16)

        for half, r in enumerate(dots(2)):
            y = (r * _sigmoid(r) * gnw_ref[:, half * nw:(half + 1) * nw]).astype(BF16)
            for hh in range(hv):
                sg_ref[0, half * hv + hh, rs, :] = y[:, hh * RET_V_DIM:(hh + 1) * RET_V_DIM]

        a, b = dots(3)
        u = (a + bglu_ref[:, :nw]) * _sigmoid(b + bglu_ref[:, nw:])
        for g in range(nw // LANES):
            u_ref[pl.ds(r0 * SUBLANES + g, rows, stride=SUBLANES), :] = (
                u[:, g * LANES:(g + 1) * LANES])

        a, b = dots(4)
        gates_ref[rs, :nw] = _sigmoid(a).astype(BF16)
        gates_ref[rs, nw:] = _sigmoid(b).astype(BF16)

        for half, r in enumerate(dots(1)):
            y = r.astype(BF16)
            for hh in range(hv):
                v_ref[0, half * hv + hh, rs, :] = y[:, hh * RET_V_DIM:(hh + 1) * RET_V_DIM]


def _in_proj(x2d, n1, w_in, cos2, sin2, b_glu, gnw, batch, seq, tm=512):
    m, d = x2d.shape
    sblocks = seq // tm
    nw = 1024
    h = RET_HEADS
    assert nw // LANES == SUBLANES and w_in.shape[1] == 10 * nw
    assert nw == h * RET_QK_DIM and 2 * nw == h * RET_V_DIM

    def head_spec(width):
        return pl.BlockSpec((1, h, tm, width), lambda i: (i // sblocks, 0, i % sblocks, 0))

    return pl.pallas_call(
        functools.partial(_in_proj_kernel, k_scale=RET_QK_DIM ** -0.5, nw=nw, row_splits=2),
        grid=(m // tm,),
        in_specs=[
            pl.BlockSpec((tm, d), lambda i: (i, 0)),
            _const_spec((1, d)),
            pl.BlockSpec(memory_space=pl.ANY),
            pl.BlockSpec((tm, RET_QK_DIM), lambda i: (i % sblocks, 0)),
            pl.BlockSpec((tm, RET_QK_DIM), lambda i: (i % sblocks, 0)),
            _const_spec((1, 2 * nw)),
            _const_spec((1, 2 * nw)),
        ],
        out_specs=[
            head_spec(RET_QK_DIM),
            head_spec(RET_QK_DIM),
            head_spec(RET_V_DIM),
            head_spec(RET_V_DIM),
            pl.BlockSpec((tm, 2 * nw), lambda i: (i, 0)),
            pl.BlockSpec((tm * SUBLANES, LANES), lambda i: (i, 0)),
        ],
        out_shape=[
            jax.ShapeDtypeStruct((batch, h, seq, RET_QK_DIM), BF16),
            jax.ShapeDtypeStruct((batch, h, seq, RET_QK_DIM), BF16),
            jax.ShapeDtypeStruct((batch, h, seq, RET_V_DIM), BF16),
            jax.ShapeDtypeStruct((batch, h, seq, RET_V_DIM), BF16),
            jax.ShapeDtypeStruct((m, 2 * nw), BF16),
            jax.ShapeDtypeStruct((m * SUBLANES, LANES), F32),
        ],
        scratch_shapes=[
            pltpu.VMEM((d, 10 * nw), BF16),
            pltpu.VMEM((2, 32, 10 * nw), F32),
            pltpu.SemaphoreType.DMA((2,)),
        ],
        compiler_params=pltpu.CompilerParams(
            dimension_semantics=("arbitrary",),
            vmem_limit_bytes=VMEM_LIMIT),
        name="in_proj",
    )(x2d, n1, w_in, cos2, sin2, b_glu, gnw)


def _ret_kernel(raw_ref, q_ref, k_ref, v_ref, o_ref,
                rst_ref, d_ref, zeta_ref, xi_ref, *, unroll):
    c = RET_CHUNK
    dk = RET_QK_DIM
    nchunks = q_ref.shape[2] // c
    lg = -jnp.exp(raw_ref[0])
    lgf = lg[0:1, :]
    lgb = lg[1:2, :]

    row = lax.broadcasted_iota(jnp.int32, (c, c), 0)
    col = lax.broadcasted_iota(jnp.int32, (c, c), 1)
    dpos = jnp.maximum(row - col, 0).astype(F32)
    dneg = jnp.maximum(col - row, 0).astype(F32)
    d_ref[...] = jnp.where(row >= col, jnp.exp(lgf * dpos), jnp.exp(lgb * dneg))

    pos = lax.broadcasted_iota(jnp.int32, (c, dk), 0).astype(F32)
    lgf_k = lgf[:, :dk]
    lgb_k = lgb[:, :dk]
    zeta_ref[0] = jnp.exp(lgf_k * (c - 1.0 - pos)).astype(BF16)
    zeta_ref[1] = jnp.exp(lgb_k * pos).astype(BF16)
    xi_ref[:, :dk] = jnp.exp(lgf_k * (pos + 1.0)).astype(BF16)
    xi_ref[:, dk:] = jnp.exp(lgb_k * (c - pos)).astype(BF16)
    g_f = jnp.exp(lgf * c)
    g_b = jnp.exp(lgb * c)

    tdims = (((0,), (0,)), ((), ()))

    def state_body(t, carry):
        rf, rb = carry
        nf = t
        nb = nchunks - 1 - t
        rst_ref[nf, 0:dk, :] = rf.astype(BF16)
        rst_ref[nb, dk:, :] = rb.astype(BF16)
        sf = pl.ds(pl.multiple_of(nf * c, c), c)
        sb = pl.ds(pl.multiple_of(nb * c, c), c)
        kzf = k_ref[0, 0, sf, :] * zeta_ref[0]
        kzb = k_ref[0, 0, sb, :] * zeta_ref[1]
        uf = lax.dot_general(kzf, v_ref[0, 0, sf, :], tdims, preferred_element_type=F32)
        ub = lax.dot_general(kzb, v_ref[0, 0, sb, :], tdims, preferred_element_type=F32)
        return g_f * rf + uf, g_b * rb + ub

    zero = jnp.zeros((dk, RET_V_DIM), F32)
    lax.fori_loop(0, nchunks, state_body, (zero, zero), unroll=unroll)

    def out_body(n, _):
        sl = pl.ds(pl.multiple_of(n * c, c), c)
        qc = q_ref[0, 0, sl, :]
        s = lax.dot_general(qc, k_ref[0, 0, sl, :], (((1,), (1,)), ((), ())),
                            preferred_element_type=F32)
        p = (s * d_ref[...]).astype(BF16)
        qx = jnp.concatenate([qc, qc], axis=1) * xi_ref[...]
        o = (jnp.dot(p, v_ref[0, 0, sl, :], preferred_element_type=F32)
             + jnp.dot(qx, rst_ref[n], preferred_element_type=F32))
        mu = jnp.mean(o, axis=-1, keepdims=True)
        oc = o - mu
        var = jnp.mean(oc * oc, axis=-1, keepdims=True)
        o_ref[0, 0, sl, :] = (oc * lax.rsqrt(var + GN_EPS)).astype(BF16)
        return 0

    lax.fori_loop(0, nchunks, out_body, 0, unroll=unroll)


def _retention(q4, k4, v4, raw_b):
    b, h, s, _ = q4.shape
    nchunks = s // RET_CHUNK

    def head_spec(width):
        return pl.BlockSpec((1, 1, s, width), lambda bi, hi: (bi, hi, 0, 0))

    return pl.pallas_call(
        functools.partial(_ret_kernel, unroll=8),
        grid=(b, h),
        in_specs=[
            pl.BlockSpec((1, 2, RET_V_DIM), lambda bi, hi: (hi, 0, 0)),
            head_spec(RET_QK_DIM),
            head_spec(RET_QK_DIM),
            head_spec(RET_V_DIM),
        ],
        out_specs=head_spec(RET_V_DIM),
        out_shape=jax.ShapeDtypeStruct((b, h, s, RET_V_DIM), BF16),
        scratch_shapes=[
            pltpu.VMEM((nchunks, 2 * RET_QK_DIM, RET_V_DIM), BF16),
            pltpu.VMEM((RET_CHUNK, RET_CHUNK), F32),
            pltpu.VMEM((2, RET_CHUNK, RET_QK_DIM), BF16),
            pltpu.VMEM((RET_CHUNK, 2 * RET_QK_DIM), BF16),
        ],
        compiler_params=pltpu.CompilerParams(
            dimension_semantics=("arbitrary", "arbitrary"),
            vmem_limit_bytes=VMEM_LIMIT),
        name="retention",
    )(raw_b, q4, k4, v4)


def _conv_kernel(prev_ref, main_ref, next_ref, cw_ref, cb_ref, out_ref, ubuf_ref, *, tblk):
    i = pl.program_id(1)
    last = pl.num_programs(1) - 1
    sl = SUBLANES
    ts = main_ref.shape[0] // sl
    ubuf_ref[0:HALO * sl, :] = jnp.where(i > 0, prev_ref[...], 0.0)
    ubuf_ref[HALO * sl:(HALO + ts) * sl, :] = main_ref[...]
    ubuf_ref[(HALO + ts) * sl:, :] = jnp.where(i < last, next_ref[...], 0.0)

    w = [cw_ref[k * sl:(k + 1) * sl, :] for k in range(CONV_WIDTH)]
    bias = cb_ref[...]

    def conv_body(bi, _):
        t0 = bi * tblk
        accs = [bias] * tblk
        for m in range(tblk + CONV_WIDTH - 1):
            start = pl.multiple_of((t0 + (HALO - CONV_HALF) + m) * sl, sl)
            tok = ubuf_ref[pl.ds(start, sl), :]
            for tt in range(tblk):
                k = m - tt
                if 0 <= k < CONV_WIDTH:
                    accs[tt] = accs[tt] + tok * w[k]
        for tt in range(tblk):
            out_ref[pl.ds(pl.multiple_of((t0 + tt) * sl, sl), sl), :] = accs[tt]
        return 0

    lax.fori_loop(0, ts // tblk, conv_body, 0)


def _conv_sums(u_tm, conv_w, conv_b, batch, seq, ts=512):
    sl = SUBLANES
    nblk = seq // ts
    hb = ts // HALO
    nhalo = batch * seq // HALO
    cw = conv_w.reshape(CONV_WIDTH * sl, LANES)
    cb = conv_b.reshape(sl, LANES)
    return pl.pallas_call(
        functools.partial(_conv_kernel, tblk=16),
        grid=(batch, nblk),
        in_specs=[
            pl.BlockSpec((HALO * sl, LANES),
                         lambda bi, i: (jnp.maximum((bi * nblk + i) * hb - 1, 0), 0)),
            pl.BlockSpec((ts * sl, LANES), lambda bi, i: (bi * nblk + i, 0)),
            pl.BlockSpec((HALO * sl, LANES),
                         lambda bi, i: (jnp.minimum((bi * nblk + i + 1) * hb, nhalo - 1), 0)),
            pl.BlockSpec((CONV_WIDTH * sl, LANES), lambda bi, i: (0, 0)),
            pl.BlockSpec((sl, LANES), lambda bi, i: (0, 0)),
        ],
        out_specs=pl.BlockSpec((ts * sl, LANES), lambda bi, i: (bi * nblk + i, 0)),
        out_shape=jax.ShapeDtypeStruct(u_tm.shape, F32),
        scratch_shapes=[pltpu.VMEM(((ts + 2 * HALO) * sl, LANES), F32)],
        compiler_params=pltpu.CompilerParams(
            dimension_semantics=("arbitrary", "arbitrary"),
            vmem_limit_bytes=VMEM_LIMIT),
        name="conv_sums",
    )(u_tm, u_tm, u_tm, cw, cb)


def _merge_kernel(x_ref, o_ref, sg_ref, cs_ref, gr_ref, gc_ref, lnw_ref, lnb_ref, wro_ref, wco_ref,
                  bco_ref, wout_ref, x2_ref, *, row_splits):
    tm = x_ref.shape[0]
    rows = tm // row_splits
    sl = SUBLANES
    for r0 in range(0, tm, rows):
        rs = slice(r0, r0 + rows)
        og = jnp.concatenate([o_ref[0, hh, rs, :] * sg_ref[0, hh, rs, :]
                              for hh in range(o_ref.shape[1])], axis=1)
        y_ret = jnp.dot(og, wro_ref[...], preferred_element_type=F32)
        a = jnp.concatenate(
            [cs_ref[pl.ds(r0 * sl + g, rows, stride=sl), :] for g in range(sl)], axis=1)
        mu = jnp.mean(a, axis=-1, keepdims=True)
        ac = a - mu
        var = jnp.mean(ac * ac, axis=-1, keepdims=True)
        y = ac * lax.rsqrt(var + GN_EPS) * lnw_ref[...] + lnb_ref[...]
        uc = (y * _sigmoid(y)).astype(BF16)
        y_conv = jnp.dot(uc, wco_ref[...], preferred_element_type=F32) + bco_ref[...]
        m = gr_ref[rs, :].astype(F32) * y_ret + gc_ref[rs, :].astype(F32) * y_conv
        x2_ref[rs, :] = x_ref[rs, :] + jnp.dot(m.astype(BF16), wout_ref[...],
                                               preferred_element_type=F32)


def _merge(x2d, o4, sg4, csum_tm, gates, ln_w, ln_b, w_ret_o, w_conv_o, b_conv_o, w_out, tm=512):
    m, d = x2d.shape
    _, h, seq, dv = o4.shape
    vw = h * dv
    sblocks = seq // tm
    return pl.pallas_call(
        functools.partial(_merge_kernel, row_splits=2),
        grid=(m // tm,),
        in_specs=[
            pl.BlockSpec((tm, d), lambda i: (i, 0)),
            pl.BlockSpec((1, h, tm, dv), lambda i: (i // sblocks, 0, i % sblocks, 0)),
            pl.BlockSpec((1, h, tm, dv), lambda i: (i // sblocks, 0, i % sblocks, 0)),
            pl.BlockSpec((tm * SUBLANES, LANES), lambda i: (i, 0)),
            pl.BlockSpec((tm, d), lambda i: (i, 0)),
            pl.BlockSpec((tm, d), lambda i: (i, 1)),
            _const_spec((1, d)),
            _const_spec((1, d)),
            _const_spec((vw, d)),
            _const_spec((d, d)),
            _const_spec((1, d)),
            _const_spec((d, d)),
        ],
        out_specs=pl.BlockSpec((tm, d), lambda i: (i, 0)),
        out_shape=jax.ShapeDtypeStruct((m, d), F32),
        compiler_params=pltpu.CompilerParams(
            dimension_semantics=("arbitrary",),
            vmem_limit_bytes=VMEM_LIMIT),
        name="merge",
    )(x2d, o4, sg4, csum_tm, gates, gates, ln_w, ln_b, w_ret_o, w_conv_o, b_conv_o, w_out)


def _mlp_kernel(x_ref, n2_ref, w1_hbm, w2_hbm, nf_ref, out_ref, w1_ref, w2_ref,
                stage1_ref, stage2_ref, sem_ref, *, final_norm, ff_chunk, row_splits):
    @pl.when(pl.program_id(0) == 0)
    def _():
        _cast_weight_once(w1_hbm, w1_ref, stage1_ref, sem_ref, chunk_rows=stage1_ref.shape[1])
        _cast_weight_once(w2_hbm, w2_ref, stage2_ref, sem_ref, chunk_rows=stage2_ref.shape[1])

    tm = x_ref.shape[0]
    rows = tm // row_splits
    for r0 in range(0, tm, rows):
        x = x_ref[r0:r0 + rows, :]
        h = (_rms_scale(x) * n2_ref[...]).astype(BF16)
        acc = x
        for c0 in range(0, w1_ref.shape[1], ff_chunk):
            a = jnp.dot(h, w1_ref[:, c0:c0 + ff_chunk], preferred_element_type=F32)
            a = jnp.square(jnp.maximum(a, 0.0)).astype(BF16)
            acc = acc + jnp.dot(a, w2_ref[c0:c0 + ff_chunk, :], preferred_element_type=F32)
        if final_norm:
            acc = _rms_scale(acc) * nf_ref[...]
        out_ref[r0:r0 + rows, :] = acc


def _mlp(x2d, n2, w1, w2, nf, final_norm, tm=1024):
    m, d = x2d.shape
    dff = w1.shape[1]
    return pl.pallas_call(
        functools.partial(_mlp_kernel, final_norm=final_norm, ff_chunk=1024, row_splits=4),
        grid=(m // tm,),
        in_specs=[
            pl.BlockSpec((tm, d), lambda i: (i, 0)),
            _const_spec((1, d)),
            pl.BlockSpec(memory_space=pl.ANY),
            pl.BlockSpec(memory_space=pl.ANY),
            _const_spec((1, d)),
        ],
        out_specs=pl.BlockSpec((tm, d), lambda i: (i, 0)),
        out_shape=jax.ShapeDtypeStruct((m, d), F32),
        scratch_shapes=[
            pltpu.VMEM((d, dff), BF16),
            pltpu.VMEM((dff, d), BF16),
            pltpu.VMEM((2, 128, dff), F32),
            pltpu.VMEM((2, 512, d), F32),
            pltpu.SemaphoreType.DMA((2,)),
        ],
        compiler_params=pltpu.CompilerParams(
            dimension_semantics=("arbitrary",),
            vmem_limit_bytes=VMEM_LIMIT),
        name="mlp",
    )(x2d, n2, w1, w2, nf)


def _rotary_tables(seq, block=64):
    half = RET_QK_DIM // 2
    lane = jnp.arange(RET_QK_DIM)
    inv_freq = ROPE_BASE ** (-(2 * (lane % half)).astype(F32) / RET_QK_DIM)
    sign = jnp.where(lane < half, -1.0, 1.0).astype(F32)
    ang_hi = (jnp.arange(seq // block, dtype=F32) * block)[:, None] * inv_freq[None, :]
    ang_lo = jnp.arange(block, dtype=F32)[:, None] * inv_freq[None, :]
    ch, sh = jnp.cos(ang_hi)[:, None, :], jnp.sin(ang_hi)[:, None, :]
    cl, sl = jnp.cos(ang_lo)[None, :, :], jnp.sin(ang_lo)[None, :, :]
    cos2 = (ch * cl - sh * sl).reshape(seq, RET_QK_DIM)
    sin2 = ((sh * cl + ch * sl) * sign).reshape(seq, RET_QK_DIM)
    return cos2, sin2


def kernel(x, norm1_w, w_in, ret_decay_raw, ret_gn_w, w_ret_o, b_glu, conv_w, conv_b, conv_ln_w,
           conv_ln_b, w_conv_o, b_conv_o, w_out, norm2_w, w_mlp1, w_mlp2, norm_f_w):
    b, s, d = x.shape
    depth = w_in.shape[0]

    cos2, sin2 = _rotary_tables(s)

    x2d = x.reshape(b * s, d)
    for l in range(depth):
        q4, k4, v4, sg4, gates, u = _in_proj(
            x2d, norm1_w[l][None, :], w_in[l], cos2, sin2, b_glu[l][None, :],
            ret_gn_w[l][None, :], b, s)
        raw_b = jnp.broadcast_to(ret_decay_raw[l].T[:, :, None], (RET_HEADS, 2, RET_V_DIM))
        o4 = _retention(q4, k4, v4, raw_b)
        csum = _conv_sums(u, conv_w[l], conv_b[l], b, s)
        x2d = _merge(x2d, o4, sg4, csum, gates, conv_ln_w[l][None, :], conv_ln_b[l][None, :],
                     w_ret_o[l].astype(BF16), w_conv_o[l].astype(BF16), b_conv_o[l][None, :],
                     w_out[l].astype(BF16))
        x2d = _mlp(x2d, norm2_w[l][None, :], w_mlp1[l], w_mlp2[l],
                   norm_f_w[None, :], final_norm=(l == depth - 1))
    return x2d.reshape(b, s, d)
```

```python
import functools

import jax
import jax.numpy as jnp
from jax import lax
from jax.experimental import pallas as pl
from jax.experimental.pallas import tpu as pltpu

F32 = jnp.float32
BF16 = jnp.bfloat16

RET_HEADS = 8
RET_QK_DIM = 128
RET_V_DIM = 256
CONV_WIDTH = 31
CONV_HALF = CONV_WIDTH // 2
ROPE_BASE = 10000.0
EPS = 1e-6
GN_EPS = 1e-5

LANES = 128
SUBLANES = 8
BF16_ROWS = 16
HALO = 16
RET_CHUNK = 256
VMEM_LIMIT = 56 * 1024 * 1024


def _sigmoid(x):
    return 1.0 / (1.0 + jnp.exp(-x))


def _rms_scale(x):
    return x * lax.rsqrt(jnp.mean(x * x, axis=-1, keepdims=True) + EPS)


def _const_spec(shape):
    return pl.BlockSpec(shape, lambda i: (0,) * len(shape), pipeline_mode=pl.Buffered(1))


def _slab_specs(weights, nsteps, index_map):
    specs = []
    for w in weights:
        rows = w.shape[0] // nsteps
        assert rows * nsteps == w.shape[0] and rows % BF16_ROWS == 0
        specs.append(pl.BlockSpec((rows, w.shape[1]), index_map))
    return specs


def _cast_weight_once(w_hbm, w_vmem, stage_ref, sem_ref, *, chunk_rows):
    nchunks = w_hbm.shape[0] // chunk_rows

    def copy(c, slot):
        return pltpu.make_async_copy(
            w_hbm.at[pl.ds(c * chunk_rows, chunk_rows), :], stage_ref.at[slot], sem_ref.at[slot])

    copy(0, 0).start()
    for c in range(nchunks):
        slot = c % 2
        if c + 1 < nchunks:
            copy(c + 1, 1 - slot).start()
        copy(c, slot).wait()
        w_vmem[c * chunk_rows:(c + 1) * chunk_rows, :] = stage_ref[slot].astype(BF16)


def _in_proj_kernel(x_ref, n1_ref, w_hbm, cos_ref, sin_ref, bglu_ref, gnw_ref,
                    q_ref, kt_ref, v_ref, sg_ref, gates_ref, u_ref, w_ref, stage_ref, sem_ref,
                    *, k_scale, nw, row_splits):
    @pl.when(pl.program_id(0) == 0)
    def _():
        _cast_weight_once(w_hbm, w_ref, stage_ref, sem_ref, chunk_rows=stage_ref.shape[1])

    tm = x_ref.shape[0]
    rows = tm // row_splits
    hv = nw // RET_V_DIM
    for r0 in range(0, tm, rows):
        rs = slice(r0, r0 + rows)
        h = (_rms_scale(x_ref[rs, :]) * n1_ref[...]).astype(BF16)

        def dots(group, h=h):
            lo = 2 * group * nw
            a = jnp.dot(h, w_ref[:, lo:lo + nw], preferred_element_type=F32)
            b = jnp.dot(h, w_ref[:, lo + nw:lo + 2 * nw], preferred_element_type=F32)
            return a, b

        a, b = dots(0)
        c = cos_ref[rs, :]
        s = sin_ref[rs, :]
        for hh in range(nw // RET_QK_DIM):
            t = a[:, hh * RET_QK_DIM:(hh + 1) * RET_QK_DIM]
            rot = pltpu.roll(t, RET_QK_DIM // 2, axis=1)
            q_ref[0, hh, rs, :] = (t * c + rot * s).astype(BF16)
        ck = c * k_scale
        sk = s * k_scale
        for hh in range(nw // RET_QK_DIM):
            t = b[:, hh * RET_QK_DIM:(hh + 1) * RET_QK_DIM]
            rot = pltpu.roll(t, RET_QK_DIM // 2, axis=1)
            kt_ref[0, hh, :, rs] = (t * ck + rot * sk).T.astype(BF16)

        for half, r in enumerate(dots(2)):
            y = (r * _sigmoid(r) * gnw_ref[:, half * nw:(half + 1) * nw]).astype(BF16)
            for hh in range(hv):
                sg_ref[0, half * hv + hh, rs, :] = y[:, hh * RET_V_DIM:(hh + 1) * RET_V_DIM]

        a, b = dots(3)
        u = (a + bglu_ref[:, :nw]) * _sigmoid(b + bglu_ref[:, nw:])
        for g in range(nw // LANES):
            u_ref[pl.ds(r0 * SUBLANES + g, rows, stride=SUBLANES), :] = (
                u[:, g * LANES:(g + 1) * LANES])

        a, b = dots(4)
        gates_ref[rs, :nw] = _sigmoid(a).astype(BF16)
        gates_ref[rs, nw:] = _sigmoid(b).astype(BF16)

        for half, r in enumerate(dots(1)):
            y = r.astype(BF16)
            for hh in range(hv):
                v_ref[0, half * hv + hh, rs, :] = y[:, hh * RET_V_DIM:(hh + 1) * RET_V_DIM]


def _in_proj(x2d, n1, w_in, cos2, sin2, b_glu, gnw, batch, seq, tm=512):
    m, d = x2d.shape
    sblocks = seq // tm
    nw = 1024
    h = RET_HEADS
    assert nw // LANES == SUBLANES and w_in.shape[1] == 10 * nw
    assert nw == h * RET_QK_DIM and 2 * nw == h * RET_V_DIM

    def head_spec(width):
        return pl.BlockSpec((1, h, tm, width), lambda i: (i // sblocks, 0, i % sblocks, 0))

    return pl.pallas_call(
        functools.partial(_in_proj_kernel, k_scale=RET_QK_DIM ** -0.5, nw=nw, row_splits=2),
        grid=(m // tm,),
        in_specs=[
            pl.BlockSpec((tm, d), lambda i: (i, 0)),
            _const_spec((1, d)),
            pl.BlockSpec(memory_space=pl.ANY),
            pl.BlockSpec((tm, RET_QK_DIM), lambda i: (i % sblocks, 0)),
            pl.BlockSpec((tm, RET_QK_DIM), lambda i: (i % sblocks, 0)),
            _const_spec((1, 2 * nw)),
            _const_spec((1, 2 * nw)),
        ],
        out_specs=[
            head_spec(RET_QK_DIM),
            pl.BlockSpec((1, h, RET_QK_DIM, tm), lambda i: (i // sblocks, 0, 0, i % sblocks)),
            head_spec(RET_V_DIM),
            head_spec(RET_V_DIM),
            pl.BlockSpec((tm, 2 * nw), lambda i: (i, 0)),
            pl.BlockSpec((tm * SUBLANES, LANES), lambda i: (i, 0)),
        ],
        out_shape=[
            jax.ShapeDtypeStruct((batch, h, seq, RET_QK_DIM), BF16),
            jax.ShapeDtypeStruct((batch, h, RET_QK_DIM, seq), BF16),
            jax.ShapeDtypeStruct((batch, h, seq, RET_V_DIM), BF16),
            jax.ShapeDtypeStruct((batch, h, seq, RET_V_DIM), BF16),
            jax.ShapeDtypeStruct((m, 2 * nw), BF16),
            jax.ShapeDtypeStruct((m * SUBLANES, LANES), F32),
        ],
        scratch_shapes=[
            pltpu.VMEM((d, 10 * nw), BF16),
            pltpu.VMEM((2, 32, 10 * nw), F32),
            pltpu.SemaphoreType.DMA((2,)),
        ],
        compiler_params=pltpu.CompilerParams(
            dimension_semantics=("arbitrary",),
            vmem_limit_bytes=VMEM_LIMIT),
        name="in_proj",
    )(x2d, n1, w_in, cos2, sin2, b_glu, gnw)


def _ret_kernel(raw_ref, q_ref, kt_ref, v_ref, o_ref, rst_ref, d_ref, zeta_ref, xi_ref):
    c = RET_CHUNK
    dk = RET_QK_DIM
    nchunks = q_ref.shape[2] // c
    lg = -jnp.exp(raw_ref[0])
    lgf = lg[0:1, :]
    lgb = lg[1:2, :]

    row = lax.broadcasted_iota(jnp.int32, (c, c), 0)
    col = lax.broadcasted_iota(jnp.int32, (c, c), 1)
    dpos = jnp.maximum(row - col, 0).astype(F32)
    dneg = jnp.maximum(col - row, 0).astype(F32)
    d_ref[...] = jnp.where(row >= col, jnp.exp(lgf * dpos), jnp.exp(lgb * dneg))

    pos = lax.broadcasted_iota(jnp.int32, (c, dk), 0).astype(F32)
    post = lax.broadcasted_iota(jnp.int32, (dk, c), 1).astype(F32)
    lgf_k = lgf[:, :dk]
    lgb_k = lgb[:, :dk]
    zeta_ref[0] = jnp.exp(lgf * (c - 1.0 - post)).astype(BF16)
    zeta_ref[1] = jnp.exp(lgb * post).astype(BF16)
    xi_ref[:, :dk] = jnp.exp(lgf_k * (pos + 1.0)).astype(BF16)
    xi_ref[:, dk:] = jnp.exp(lgb_k * (c - pos)).astype(BF16)
    g_f = jnp.exp(lgf * c)
    g_b = jnp.exp(lgb * c)

    def state_body(t, carry):
        rf, rb = carry
        nf = t
        nb = nchunks - 1 - t
        rst_ref[nf, 0:dk, :] = rf.astype(BF16)
        rst_ref[nb, dk:, :] = rb.astype(BF16)
        sf = pl.ds(pl.multiple_of(nf * c, c), c)
        sb = pl.ds(pl.multiple_of(nb * c, c), c)
        kzf = kt_ref[0, 0, :, sf] * zeta_ref[0]
        kzb = kt_ref[0, 0, :, sb] * zeta_ref[1]
        uf = jnp.dot(kzf, v_ref[0, 0, sf, :], preferred_element_type=F32)
        ub = jnp.dot(kzb, v_ref[0, 0, sb, :], preferred_element_type=F32)
        return g_f * rf + uf, g_b * rb + ub

    zero = jnp.zeros((dk, RET_V_DIM), F32)
    lax.fori_loop(0, nchunks, state_body, (zero, zero), unroll=True)

    def out_body(n, _):
        sl = pl.ds(pl.multiple_of(n * c, c), c)
        qc = q_ref[0, 0, sl, :]
        s = jnp.dot(qc, kt_ref[0, 0, :, sl], preferred_element_type=F32)
        p = (s * d_ref[...]).astype(BF16)
        qx = jnp.concatenate([qc, qc], axis=1) * xi_ref[...]
        o = (jnp.dot(p, v_ref[0, 0, sl, :], preferred_element_type=F32)
             + jnp.dot(qx, rst_ref[n], preferred_element_type=F32))
        mu = jnp.mean(o, axis=-1, keepdims=True)
        oc = o - mu
        var = jnp.mean(oc * oc, axis=-1, keepdims=True)
        o_ref[0, 0, sl, :] = (oc * lax.rsqrt(var + GN_EPS)).astype(BF16)
        return 0

    lax.fori_loop(0, nchunks, out_body, 0, unroll=True)


def _retention(q4, kt4, v4, raw_b):
    b, h, s, _ = q4.shape
    nchunks = s // RET_CHUNK

    def head_spec(width):
        return pl.BlockSpec((1, 1, s, width), lambda bi, hi: (bi, hi, 0, 0))

    return pl.pallas_call(
        _ret_kernel,
        grid=(b, h),
        in_specs=[
            pl.BlockSpec((1, 2, RET_V_DIM), lambda bi, hi: (hi, 0, 0)),
            head_spec(RET_QK_DIM),
            pl.BlockSpec((1, 1, RET_QK_DIM, s), lambda bi, hi: (bi, hi, 0, 0)),
            head_spec(RET_V_DIM),
        ],
        out_specs=head_spec(RET_V_DIM),
        out_shape=jax.ShapeDtypeStruct((b, h, s, RET_V_DIM), BF16),
        scratch_shapes=[
            pltpu.VMEM((nchunks, 2 * RET_QK_DIM, RET_V_DIM), BF16),
            pltpu.VMEM((RET_CHUNK, RET_CHUNK), F32),
            pltpu.VMEM((2, RET_QK_DIM, RET_CHUNK), BF16),
            pltpu.VMEM((RET_CHUNK, 2 * RET_QK_DIM), BF16),
        ],
        compiler_params=pltpu.CompilerParams(
            dimension_semantics=("arbitrary", "arbitrary"),
            vmem_limit_bytes=VMEM_LIMIT),
        name="retention",
    )(raw_b, q4, kt4, v4)


def _conv_kernel(prev_ref, main_ref, next_ref, cw_ref, cb_ref, *rest, tblk, n_cast):
    cast_in, out_ref, cast_out, edge_ref = (rest[:n_cast], rest[n_cast], rest[n_cast + 1:-1],
                                             rest[-1])
    for src, dst in zip(cast_in, cast_out):
        dst[...] = src[...].astype(BF16)

    i = pl.program_id(1)
    last = pl.num_programs(1) - 1
    sl = SUBLANES
    ts = main_ref.shape[0] // sl
    span = tblk + CONV_WIDTH - 1
    nblocks = ts // tblk
    assert span - tblk <= 2 * HALO and nblocks >= 3

    w = [cw_ref[k * sl:(k + 1) * sl, :] for k in range(CONV_WIDTH)]
    bias = cb_ref[...]

    def conv_block(src_ref, src_tok0, out_tok0):
        accs = [bias] * tblk
        for m in range(span):
            start = (src_tok0 + m) * sl
            if not isinstance(start, int):
                start = pl.multiple_of(start, sl)
            tok = src_ref[pl.ds(start, sl), :]
            for tt in range(tblk):
                k = m - tt
                if 0 <= k < CONV_WIDTH:
                    accs[tt] = accs[tt] + tok * w[k]
        for tt in range(tblk):
            o = (out_tok0 + tt) * sl
            if not isinstance(o, int):
                o = pl.multiple_of(o, sl)
            out_ref[pl.ds(o, sl), :] = accs[tt]

    edge_ref[0:HALO * sl, :] = jnp.where(i > 0, prev_ref[...], 0.0)
    edge_ref[HALO * sl:(HALO + span) * sl, :] = main_ref[0:span * sl, :]
    conv_block(edge_ref, HALO - CONV_HALF, 0)
    e1 = HALO + span
    edge_ref[e1 * sl:(e1 + span) * sl, :] = main_ref[(ts - span) * sl:, :]
    edge_ref[(e1 + span) * sl:, :] = jnp.where(i < last, next_ref[...], 0.0)
    conv_block(edge_ref, e1 + span - tblk - CONV_HALF, ts - tblk)

    def conv_body(bi, _):
        t0 = bi * tblk
        conv_block(main_ref, t0 - CONV_HALF, t0)
        return 0

    lax.fori_loop(1, nblocks - 1, conv_body, 0)


def _conv_sums(u_tm, conv_w, conv_b, cast_weights, batch, seq, ts=512, tblk=16):
    sl = SUBLANES
    nblk = seq // ts
    hb = ts // HALO
    nhalo = batch * seq // HALO
    cw = conv_w.reshape(CONV_WIDTH * sl, LANES)
    cb = conv_b.reshape(sl, LANES)
    slabs = _slab_specs(cast_weights, batch * nblk, lambda bi, i: (bi * nblk + i, 0))
    return pl.pallas_call(
        functools.partial(_conv_kernel, tblk=tblk, n_cast=len(cast_weights)),
        grid=(batch, nblk),
        in_specs=[
            pl.BlockSpec((HALO * sl, LANES),
                         lambda bi, i: (jnp.maximum((bi * nblk + i) * hb - 1, 0), 0)),
            pl.BlockSpec((ts * sl, LANES), lambda bi, i: (bi * nblk + i, 0)),
            pl.BlockSpec((HALO * sl, LANES),
                         lambda bi, i: (jnp.minimum((bi * nblk + i + 1) * hb, nhalo - 1), 0)),
            pl.BlockSpec((CONV_WIDTH * sl, LANES), lambda bi, i: (0, 0)),
            pl.BlockSpec((sl, LANES), lambda bi, i: (0, 0)),
        ] + slabs,
        out_specs=[pl.BlockSpec((ts * sl, LANES), lambda bi, i: (bi * nblk + i, 0))] + slabs,
        out_shape=[jax.ShapeDtypeStruct(u_tm.shape, F32)]
        + [jax.ShapeDtypeStruct(w.shape, BF16) for w in cast_weights],
        scratch_shapes=[pltpu.VMEM((2 * (HALO + tblk + CONV_WIDTH - 1) * sl, LANES), F32)],
        compiler_params=pltpu.CompilerParams(
            dimension_semantics=("arbitrary", "arbitrary"),
            vmem_limit_bytes=VMEM_LIMIT),
        name="conv_sums",
    )(u_tm, u_tm, u_tm, cw, cb, *cast_weights)


def _merge_kernel(x_ref, o_ref, sg_ref, cs_ref, gr_ref, gc_ref, lnw_ref, lnb_ref, wro_ref, wco_ref,
                  bco_ref, wout_ref, *rest, row_splits, n_cast):
    cast_in, x2_ref, cast_out = rest[:n_cast], rest[n_cast], rest[n_cast + 1:]
    for src, dst in zip(cast_in, cast_out):
        dst[...] = src[...].astype(BF16)

    tm = x_ref.shape[0]
    rows = tm // row_splits
    sl = SUBLANES
    for r0 in range(0, tm, rows):
        rs = slice(r0, r0 + rows)
        og = jnp.concatenate([o_ref[0, hh, rs, :] * sg_ref[0, hh, rs, :]
                              for hh in range(o_ref.shape[1])], axis=1)
        y_ret = jnp.dot(og, wro_ref[...], preferred_element_type=F32)
        a = jnp.concatenate(
            [cs_ref[pl.ds(r0 * sl + g, rows, stride=sl), :] for g in range(sl)], axis=1)
        mu = jnp.mean(a, axis=-1, keepdims=True)
        ac = a - mu
        var = jnp.mean(ac * ac, axis=-1, keepdims=True)
        y = ac * lax.rsqrt(var + GN_EPS) * lnw_ref[...] + lnb_ref[...]
        uc = (y * _sigmoid(y)).astype(BF16)
        y_conv = jnp.dot(uc, wco_ref[...], preferred_element_type=F32) + bco_ref[...]
        m = gr_ref[rs, :].astype(F32) * y_ret + gc_ref[rs, :].astype(F32) * y_conv
        x2_ref[rs, :] = x_ref[rs, :] + jnp.dot(m.astype(BF16), wout_ref[...],
                                               preferred_element_type=F32)


def _merge(x2d, o4, sg4, csum_tm, gates, ln_w, ln_b, w_ret_o, w_conv_o, b_conv_o, w_out,
           cast_weights, tm=512):
    m, d = x2d.shape
    _, h, seq, dv = o4.shape
    vw = h * dv
    sblocks = seq // tm
    slabs = _slab_specs(cast_weights, m // tm, lambda i: (i, 0))
    return pl.pallas_call(
        functools.partial(_merge_kernel, row_splits=2, n_cast=len(cast_weights)),
        grid=(m // tm,),
        in_specs=[
            pl.BlockSpec((tm, d), lambda i: (i, 0)),
            pl.BlockSpec((1, h, tm, dv), lambda i: (i // sblocks, 0, i % sblocks, 0)),
            pl.BlockSpec((1, h, tm, dv), lambda i: (i // sblocks, 0, i % sblocks, 0)),
            pl.BlockSpec((tm * SUBLANES, LANES), lambda i: (i, 0)),
            pl.BlockSpec((tm, d), lambda i: (i, 0)),
            pl.BlockSpec((tm, d), lambda i: (i, 1)),
            _const_spec((1, d)),
            _const_spec((1, d)),
            _const_spec((vw, d)),
            _const_spec((d, d)),
            _const_spec((1, d)),
            _const_spec((d, d)),
        ] + slabs,
        out_specs=[pl.BlockSpec((tm, d), lambda i: (i, 0))] + slabs,
        out_shape=[jax.ShapeDtypeStruct((m, d), F32)]
        + [jax.ShapeDtypeStruct(w.shape, BF16) for w in cast_weights],
        compiler_params=pltpu.CompilerParams(
            dimension_semantics=("arbitrary",),
            vmem_limit_bytes=VMEM_LIMIT),
        name="merge",
    )(x2d, o4, sg4, csum_tm, gates, gates, ln_w, ln_b, w_ret_o, w_conv_o, b_conv_o, w_out,
      *cast_weights)


def _mlp_kernel(x_ref, n2_ref, w1_ref, w2_ref, nf_ref, out_ref, *, final_norm, ff_chunk,
                row_splits):
    tm = x_ref.shape[0]
    rows = tm // row_splits
    for r0 in range(0, tm, rows):
        x = x_ref[r0:r0 + rows, :]
        h = (_rms_scale(x) * n2_ref[...]).astype(BF16)
        acc = x
        for c0 in range(0, w1_ref.shape[1], ff_chunk):
            a = jnp.dot(h, w1_ref[:, c0:c0 + ff_chunk], preferred_element_type=F32)
            a = jnp.square(jnp.maximum(a, 0.0)).astype(BF16)
            acc = acc + jnp.dot(a, w2_ref[c0:c0 + ff_chunk, :], preferred_element_type=F32)
        if final_norm:
            acc = _rms_scale(acc) * nf_ref[...]
        out_ref[r0:r0 + rows, :] = acc


def _mlp(x2d, n2, w1, w2, nf, final_norm, tm=1024):
    m, d = x2d.shape
    dff = w1.shape[1]
    return pl.pallas_call(
        functools.partial(_mlp_kernel, final_norm=final_norm, ff_chunk=1024, row_splits=4),
        grid=(m // tm,),
        in_specs=[
            pl.BlockSpec((tm, d), lambda i: (i, 0)),
            _const_spec((1, d)),
            _const_spec((d, dff)),
            _const_spec((dff, d)),
            _const_spec((1, d)),
        ],
        out_specs=pl.BlockSpec((tm, d), lambda i: (i, 0)),
        out_shape=jax.ShapeDtypeStruct((m, d), F32),
        compiler_params=pltpu.CompilerParams(
            dimension_semantics=("arbitrary",),
            vmem_limit_bytes=VMEM_LIMIT),
        name="mlp",
    )(x2d, n2, w1, w2, nf)


def _rotary_tables(seq, block=64):
    half = RET_QK_DIM // 2
    lane = jnp.arange(RET_QK_DIM)
    inv_freq = ROPE_BASE ** (-(2 * (lane % half)).astype(F32) / RET_QK_DIM)
    sign = jnp.where(lane < half, -1.0, 1.0).astype(F32)
    ang_hi = (jnp.arange(seq // block, dtype=F32) * block)[:, None] * inv_freq[None, :]
    ang_lo = jnp.arange(block, dtype=F32)[:, None] * inv_freq[None, :]
    ch, sh = jnp.cos(ang_hi)[:, None, :], jnp.sin(ang_hi)[:, None, :]
    cl, sl = jnp.cos(ang_lo)[None, :, :], jnp.sin(ang_lo)[None, :, :]
    cos2 = (ch * cl - sh * sl).reshape(seq, RET_QK_DIM)
    sin2 = ((sh * cl + ch * sl) * sign).reshape(seq, RET_QK_DIM)
    return cos2, sin2


def kernel(x, norm1_w, w_in, ret_decay_raw, ret_gn_w, w_ret_o, b_glu, conv_w, conv_b, conv_ln_w,
           conv_ln_b, w_conv_o, b_conv_o, w_out, norm2_w, w_mlp1, w_mlp2, norm_f_w):
    b, s, d = x.shape
    depth = w_in.shape[0]

    cos2, sin2 = _rotary_tables(s)

    x2d = x.reshape(b * s, d)
    for l in range(depth):
        q4, kt4, v4, sg4, gates, u = _in_proj(
            x2d, norm1_w[l][None, :], w_in[l], cos2, sin2, b_glu[l][None, :],
            ret_gn_w[l][None, :], b, s)
        raw_b = jnp.broadcast_to(ret_decay_raw[l].T[:, :, None], (RET_HEADS, 2, RET_V_DIM))
        o4 = _retention(q4, kt4, v4, raw_b)
        csum, wro_b, wco_b, wout_b = _conv_sums(
            u, conv_w[l], conv_b[l], (w_ret_o[l], w_conv_o[l], w_out[l]), b, s)
        x2d, w1_b, w2_b = _merge(
            x2d, o4, sg4, csum, gates, conv_ln_w[l][None, :], conv_ln_b[l][None, :],
            wro_b, wco_b, b_conv_o[l][None, :], wout_b, (w_mlp1[l], w_mlp2[l]))
        x2d = _mlp(x2d, norm2_w[l][None, :], w1_b, w2_b, norm_f_w[None, :],
                   final_norm=(l == depth - 1))
    return x2d.reshape(b, s, d)
```

```python
import functools

import jax
import jax.numpy as jnp
from jax import lax
from jax.experimental import pallas as pl
from jax.experimental.pallas import tpu as pltpu

F32 = jnp.float32
BF16 = jnp.bfloat16

RET_HEADS = 8
RET_QK_DIM = 128
RET_V_DIM = 256
CONV_WIDTH = 31
CONV_HALF = CONV_WIDTH // 2
ROPE_BASE = 10000.0
EPS = 1e-6
GN_EPS = 1e-5

LANES = 128
SUBLANES = 8
BF16_ROWS = 16
HALO = 16
RET_CHUNK = 256
VMEM_LIMIT = 56 * 1024 * 1024


def _sigmoid(x):
    return 0.5 * jnp.tanh(0.5 * x) + 0.5


def _rms_scale(x):
    return x * lax.rsqrt(jnp.mean(x * x, axis=-1, keepdims=True) + EPS)


def _const_spec(shape):
    return pl.BlockSpec(shape, lambda i: (0,) * len(shape), pipeline_mode=pl.Buffered(1))


def _slab_specs(weights, nsteps, index_map):
    specs = []
    for w in weights:
        rows = w.shape[0] // nsteps
        assert rows * nsteps == w.shape[0] and rows % BF16_ROWS == 0
        specs.append(pl.BlockSpec((rows, w.shape[1]), index_map))
    return specs


def _cast_weight_once(w_hbm, w_vmem, stage_ref, sem_ref):
    nslots, chunk_rows = stage_ref.shape[0], stage_ref.shape[1]
    nchunks = w_hbm.shape[0] // chunk_rows

    def copy(c):
        slot = c % nslots
        return pltpu.make_async_copy(
            w_hbm.at[pl.ds(c * chunk_rows, chunk_rows), :], stage_ref.at[slot], sem_ref.at[slot])

    for c in range(min(nslots - 1, nchunks)):
        copy(c).start()
    for c in range(nchunks):
        if c + nslots - 1 < nchunks:
            copy(c + nslots - 1).start()
        copy(c).wait()
        w_vmem[c * chunk_rows:(c + 1) * chunk_rows, :] = stage_ref[c % nslots].astype(BF16)


def _in_proj_kernel(x_ref, n1_ref, w_hbm, cos_ref, sin_ref, bglu_ref, gnw_ref,
                    q_ref, kt_ref, v_ref, sg_ref, gates_ref, u_ref, w_ref, stage_ref, sem_ref,
                    *, k_scale, nw, row_splits):
    @pl.when(pl.program_id(0) == 0)
    def _():
        _cast_weight_once(w_hbm, w_ref, stage_ref, sem_ref)

    tm = x_ref.shape[0]
    rows = tm // row_splits
    hv = nw // RET_V_DIM
    for r0 in range(0, tm, rows):
        rs = slice(r0, r0 + rows)
        h = (_rms_scale(x_ref[rs, :]) * n1_ref[...]).astype(BF16)

        def dots(group, h=h):
            lo = 2 * group * nw
            a = jnp.dot(h, w_ref[:, lo:lo + nw], preferred_element_type=F32)
            b = jnp.dot(h, w_ref[:, lo + nw:lo + 2 * nw], preferred_element_type=F32)
            return a, b

        a, b = dots(0)
        c = cos_ref[rs, :]
        s = sin_ref[rs, :]
        for hh in range(nw // RET_QK_DIM):
            t = a[:, hh * RET_QK_DIM:(hh + 1) * RET_QK_DIM]
            rot = pltpu.roll(t, RET_QK_DIM // 2, axis=1)
            q_ref[0, hh, rs, :] = (t * c + rot * s).astype(BF16)
        ck = c * k_scale
        sk = s * k_scale
        for hh in range(nw // RET_QK_DIM):
            t = b[:, hh * RET_QK_DIM:(hh + 1) * RET_QK_DIM]
            rot = pltpu.roll(t, RET_QK_DIM // 2, axis=1)
            kt_ref[0, hh, :, rs] = (t * ck + rot * sk).T.astype(BF16)

        for half, r in enumerate(dots(2)):
            y = (r * _sigmoid(r) * gnw_ref[:, half * nw:(half + 1) * nw]).astype(BF16)
            for hh in range(hv):
                sg_ref[0, half * hv + hh, rs, :] = y[:, hh * RET_V_DIM:(hh + 1) * RET_V_DIM]

        a, b = dots(3)
        u = (a + bglu_ref[:, :nw]) * _sigmoid(b + bglu_ref[:, nw:])
        for g in range(nw // LANES):
            u_ref[pl.ds(r0 * SUBLANES + g, rows, stride=SUBLANES), :] = (
                u[:, g * LANES:(g + 1) * LANES])

        a, b = dots(4)
        gates_ref[rs, :nw] = _sigmoid(a).astype(BF16)
        gates_ref[rs, nw:] = _sigmoid(b).astype(BF16)

        for half, r in enumerate(dots(1)):
            y = r.astype(BF16)
            for hh in range(hv):
                v_ref[0, half * hv + hh, rs, :] = y[:, hh * RET_V_DIM:(hh + 1) * RET_V_DIM]


def _in_proj(x2d, n1, w_in, cos2, sin2, b_glu, gnw, batch, seq, tm=512):
    m, d = x2d.shape
    sblocks = seq // tm
    nw = 1024
    h = RET_HEADS
    assert nw // LANES == SUBLANES and w_in.shape[1] == 10 * nw
    assert nw == h * RET_QK_DIM and 2 * nw == h * RET_V_DIM

    def head_spec(width):
        return pl.BlockSpec((1, h, tm, width), lambda i: (i // sblocks, 0, i % sblocks, 0))

    return pl.pallas_call(
        functools.partial(_in_proj_kernel, k_scale=RET_QK_DIM ** -0.5, nw=nw, row_splits=2),
        grid=(m // tm,),
        in_specs=[
            pl.BlockSpec((tm, d), lambda i: (i, 0)),
            _const_spec((1, d)),
            pl.BlockSpec(memory_space=pl.ANY),
            pl.BlockSpec((tm, RET_QK_DIM), lambda i: (i % sblocks, 0)),
            pl.BlockSpec((tm, RET_QK_DIM), lambda i: (i % sblocks, 0)),
            _const_spec((1, 2 * nw)),
            _const_spec((1, 2 * nw)),
        ],
        out_specs=[
            head_spec(RET_QK_DIM),
            pl.BlockSpec((1, h, RET_QK_DIM, tm), lambda i: (i // sblocks, 0, 0, i % sblocks)),
            head_spec(RET_V_DIM),
            head_spec(RET_V_DIM),
            pl.BlockSpec((tm, 2 * nw), lambda i: (i, 0)),
            pl.BlockSpec((tm * SUBLANES, LANES), lambda i: (i, 0)),
        ],
        out_shape=[
            jax.ShapeDtypeStruct((batch, h, seq, RET_QK_DIM), BF16),
            jax.ShapeDtypeStruct((batch, h, RET_QK_DIM, seq), BF16),
            jax.ShapeDtypeStruct((batch, h, seq, RET_V_DIM), BF16),
            jax.ShapeDtypeStruct((batch, h, seq, RET_V_DIM), BF16),
            jax.ShapeDtypeStruct((m, 2 * nw), BF16),
            jax.ShapeDtypeStruct((m * SUBLANES, LANES), F32),
        ],
        scratch_shapes=[
            pltpu.VMEM((d, 10 * nw), BF16),
            pltpu.VMEM((4, 16, 10 * nw), F32),
            pltpu.SemaphoreType.DMA((4,)),
        ],
        compiler_params=pltpu.CompilerParams(
            dimension_semantics=("arbitrary",),
            vmem_limit_bytes=VMEM_LIMIT),
        name="in_proj",
    )(x2d, n1, w_in, cos2, sin2, b_glu, gnw)


def _ret_kernel(raw_ref, q_ref, kt_ref, v_ref, o_ref, rst_ref, d_ref, zeta_ref, xi_ref):
    c = RET_CHUNK
    dk = RET_QK_DIM
    nchunks = q_ref.shape[2] // c
    lg = -jnp.exp(raw_ref[0])
    lgf = lg[0:1, :]
    lgb = lg[1:2, :]

    row = lax.broadcasted_iota(jnp.int32, (c, c), 0)
    col = lax.broadcasted_iota(jnp.int32, (c, c), 1)
    dpos = jnp.maximum(row - col, 0).astype(F32)
    dneg = jnp.maximum(col - row, 0).astype(F32)
    d_ref[...] = jnp.where(row >= col, jnp.exp(lgf * dpos), jnp.exp(lgb * dneg))

    pos = lax.broadcasted_iota(jnp.int32, (c, dk), 0).astype(F32)
    post = lax.broadcasted_iota(jnp.int32, (dk, c), 1).astype(F32)
    lgf_k = lgf[:, :dk]
    lgb_k = lgb[:, :dk]
    zeta_ref[0] = jnp.exp(lgf * (c - 1.0 - post)).astype(BF16)
    zeta_ref[1] = jnp.exp(lgb * post).astype(BF16)
    xi_ref[:, :dk] = jnp.exp(lgf_k * (pos + 1.0)).astype(BF16)
    xi_ref[:, dk:] = jnp.exp(lgb_k * (c - pos)).astype(BF16)
    g_f = jnp.exp(lgf * c)
    g_b = jnp.exp(lgb * c)

    def state_body(t, carry):
        rf, rb = carry
        nf = t
        nb = nchunks - 1 - t
        rst_ref[nf, 0:dk, :] = rf.astype(BF16)
        rst_ref[nb, dk:, :] = rb.astype(BF16)
        sf = pl.ds(pl.multiple_of(nf * c, c), c)
        sb = pl.ds(pl.multiple_of(nb * c, c), c)
        kzf = kt_ref[0, 0, :, sf] * zeta_ref[0]
        kzb = kt_ref[0, 0, :, sb] * zeta_ref[1]
        uf = jnp.dot(kzf, v_ref[0, 0, sf, :], preferred_element_type=F32)
        ub = jnp.dot(kzb, v_ref[0, 0, sb, :], preferred_element_type=F32)
        return g_f * rf + uf, g_b * rb + ub

    zero = jnp.zeros((dk, RET_V_DIM), F32)
    lax.fori_loop(0, nchunks, state_body, (zero, zero), unroll=True)

    def out_body(n, _):
        sl = pl.ds(pl.multiple_of(n * c, c), c)
        qc = q_ref[0, 0, sl, :]
        s = jnp.dot(qc, kt_ref[0, 0, :, sl], preferred_element_type=F32)
        p = (s * d_ref[...]).astype(BF16)
        qx = jnp.concatenate([qc, qc], axis=1) * xi_ref[...]
        o = (jnp.dot(p, v_ref[0, 0, sl, :], preferred_element_type=F32)
             + jnp.dot(qx, rst_ref[n], preferred_element_type=F32))
        mu = jnp.mean(o, axis=-1, keepdims=True)
        oc = o - mu
        var = jnp.mean(oc * oc, axis=-1, keepdims=True)
        o_ref[0, 0, sl, :] = (oc * lax.rsqrt(var + GN_EPS)).astype(BF16)
        return 0

    lax.fori_loop(0, nchunks, out_body, 0, unroll=True)


def _retention(q4, kt4, v4, raw_b):
    b, h, s, _ = q4.shape
    nchunks = s // RET_CHUNK

    def head_spec(width):
        return pl.BlockSpec((1, 1, s, width), lambda bi, hi: (bi, hi, 0, 0))

    return pl.pallas_call(
        _ret_kernel,
        grid=(b, h),
        in_specs=[
            pl.BlockSpec((1, 2, RET_V_DIM), lambda bi, hi: (hi, 0, 0)),
            head_spec(RET_QK_DIM),
            pl.BlockSpec((1, 1, RET_QK_DIM, s), lambda bi, hi: (bi, hi, 0, 0)),
            head_spec(RET_V_DIM),
        ],
        out_specs=head_spec(RET_V_DIM),
        out_shape=jax.ShapeDtypeStruct((b, h, s, RET_V_DIM), BF16),
        scratch_shapes=[
            pltpu.VMEM((nchunks, 2 * RET_QK_DIM, RET_V_DIM), BF16),
            pltpu.VMEM((RET_CHUNK, RET_CHUNK), F32),
            pltpu.VMEM((2, RET_QK_DIM, RET_CHUNK), BF16),
            pltpu.VMEM((RET_CHUNK, 2 * RET_QK_DIM), BF16),
        ],
        compiler_params=pltpu.CompilerParams(
            dimension_semantics=("arbitrary", "arbitrary"),
            vmem_limit_bytes=VMEM_LIMIT),
        name="retention",
    )(raw_b, q4, kt4, v4)


def _conv_kernel(prev_ref, main_ref, next_ref, cw_ref, cb_ref, *rest, pblk, group, n_cast):
    cast_in, out_ref = rest[:n_cast], rest[n_cast]
    cast_out, (pe_ref, po_ref) = rest[n_cast + 1:-2], rest[-2:]
    for src, dst in zip(cast_in, cast_out):
        dst[...] = src[...].astype(BF16)

    i = pl.program_id(1)
    last = pl.num_programs(1) - 1
    sl = SUBLANES
    ps = 2 * sl
    ts = main_ref.shape[0] // sl
    assert ps == BF16_ROWS and ts % (2 * pblk) == 0 and (HALO - CONV_HALF) % 2 == 1

    hp = HALO // 2
    prev = jnp.where(i > 0, prev_ref[...], 0.0)
    nxt = jnp.where(i < last, next_ref[...], 0.0)

    def put(dst_ref, j, lo, hi=None):
        pair = lo if hi is None else jnp.concatenate([lo, hi], axis=0)
        dst_ref[j * ps:(j + 1) * ps, :] = pair.astype(BF16)

    for j in range(hp):
        put(pe_ref, j, prev[j * ps:(j + 1) * ps, :])
        put(pe_ref, hp + ts // 2 + j, nxt[j * ps:(j + 1) * ps, :])
    for j in range(hp - 1):
        put(po_ref, j, prev[j * ps + sl:(j + 1) * ps + sl, :])
        put(po_ref, hp + ts // 2 + j, nxt[j * ps + sl:(j + 1) * ps + sl, :])
    put(po_ref, hp - 1, prev[(HALO - 1) * sl:, :], main_ref[0:sl, :])
    put(po_ref, hp - 1 + ts // 2, main_ref[(ts - 1) * sl:, :], nxt[0:sl, :])

    pack_unroll = 8

    def pack_body(jb, _):
        base = pl.multiple_of(jb * pack_unroll * ps, pack_unroll * ps)
        for jj in range(pack_unroll):
            r = base + jj * ps
            pe_ref[pl.ds(HALO * sl + r, ps), :] = main_ref[pl.ds(r, ps), :].astype(BF16)
            po_ref[pl.ds(HALO * sl + r, ps), :] = main_ref[pl.ds(r + sl, ps), :].astype(BF16)
        return 0

    lax.fori_loop(0, (ts // 2) // pack_unroll - 1, pack_body, 0)
    tail0 = ((ts // 2) // pack_unroll - 1) * pack_unroll
    for j in range(tail0, ts // 2):
        put(pe_ref, hp + j, main_ref[j * ps:(j + 1) * ps, :])
        if j < ts // 2 - 1:
            put(po_ref, hp + j, main_ref[j * ps + sl:(j + 1) * ps + sl, :])

    w16 = []
    for k in range(CONV_WIDTH):
        wk = cw_ref[k * sl:(k + 1) * sl, :]
        w16.append(jnp.concatenate([wk, wk], axis=0).astype(BF16))
    bias16 = jnp.concatenate([cb_ref[...], cb_ref[...]], axis=0)
    half_taps = CONV_WIDTH // 2

    def conv_body(bi, _):
        p0 = bi * pblk
        acc = [bias16] * pblk
        part = [None] * pblk
        count = [0] * pblk
        for m in range(pblk + half_taps):
            row = pl.multiple_of((p0 + m) * ps, ps)
            srcs = {}
            for pp in range(pblk):
                for k, name in ((2 * (m - pp) - 1, "pe"), (2 * (m - pp), "po")):
                    if not 0 <= k < CONV_WIDTH:
                        continue
                    if name not in srcs:
                        ref = pe_ref if name == "pe" else po_ref
                        srcs[name] = ref[pl.ds(row, ps), :]
                    term = srcs[name] * w16[k]
                    part[pp] = term if part[pp] is None else part[pp] + term
                    count[pp] += 1
                    if count[pp] == group or k == CONV_WIDTH - 1:
                        acc[pp] = acc[pp] + part[pp].astype(F32)
                        part[pp] = None
                        count[pp] = 0
        for pp in range(pblk):
            out_ref[pl.ds(pl.multiple_of((p0 + pp) * ps, ps), ps), :] = acc[pp]
        return 0

    lax.fori_loop(0, ts // (2 * pblk), conv_body, 0)


def _conv_sums(u_tm, conv_w, conv_b, cast_weights, batch, seq, ts=1024):
    sl = SUBLANES
    nblk = seq // ts
    hb = ts // HALO
    nhalo = batch * seq // HALO
    cw = conv_w.reshape(CONV_WIDTH * sl, LANES)
    cb = conv_b.reshape(sl, LANES)
    slabs = _slab_specs(cast_weights, batch * nblk, lambda bi, i: (bi * nblk + i, 0))
    return pl.pallas_call(
        functools.partial(_conv_kernel, pblk=8, group=4, n_cast=len(cast_weights)),
        grid=(batch, nblk),
        in_specs=[
            pl.BlockSpec((HALO * sl, LANES),
                         lambda bi, i: (jnp.maximum((bi * nblk + i) * hb - 1, 0), 0)),
            pl.BlockSpec((ts * sl, LANES), lambda bi, i: (bi * nblk + i, 0)),
            pl.BlockSpec((HALO * sl, LANES),
                         lambda bi, i: (jnp.minimum((bi * nblk + i + 1) * hb, nhalo - 1), 0)),
            pl.BlockSpec((CONV_WIDTH * sl, LANES), lambda bi, i: (0, 0)),
            pl.BlockSpec((sl, LANES), lambda bi, i: (0, 0)),
        ] + slabs,
        out_specs=[pl.BlockSpec((ts * sl, LANES), lambda bi, i: (bi * nblk + i, 0))] + slabs,
        out_shape=[jax.ShapeDtypeStruct(u_tm.shape, F32)]
        + [jax.ShapeDtypeStruct(w.shape, BF16) for w in cast_weights],
        scratch_shapes=[
            pltpu.VMEM(((ts + 2 * HALO) * sl, LANES), BF16),
            pltpu.VMEM(((ts + 2 * HALO) * sl, LANES), BF16),
        ],
        compiler_params=pltpu.CompilerParams(
            dimension_semantics=("arbitrary", "arbitrary"),
            vmem_limit_bytes=VMEM_LIMIT),
        name="conv_sums",
    )(u_tm, u_tm, u_tm, cw, cb, *cast_weights)


def _merge_kernel(x_ref, o_ref, sg_ref, cs_ref, gr_ref, gc_ref, lnw_ref, lnb_ref, wro_ref, wco_ref,
                  bco_ref, wout_ref, *rest, row_splits, n_cast):
    cast_in, x2_ref, cast_out = rest[:n_cast], rest[n_cast], rest[n_cast + 1:]
    for src, dst in zip(cast_in, cast_out):
        dst[...] = src[...].astype(BF16)

    tm = x_ref.shape[0]
    rows = tm // row_splits
    sl = SUBLANES
    for r0 in range(0, tm, rows):
        rs = slice(r0, r0 + rows)
        og = jnp.concatenate([o_ref[0, hh, rs, :] * sg_ref[0, hh, rs, :]
                              for hh in range(o_ref.shape[1])], axis=1)
        y_ret = jnp.dot(og, wro_ref[...], preferred_element_type=F32)
        a = jnp.concatenate(
            [cs_ref[pl.ds(r0 * sl + g, rows, stride=sl), :] for g in range(sl)], axis=1)
        mu = jnp.mean(a, axis=-1, keepdims=True)
        ac = a - mu
        var = jnp.mean(ac * ac, axis=-1, keepdims=True)
        y = ac * lax.rsqrt(var + GN_EPS) * lnw_ref[...] + lnb_ref[...]
        uc = (y * _sigmoid(y)).astype(BF16)
        y_conv = jnp.dot(uc, wco_ref[...], preferred_element_type=F32) + bco_ref[...]
        m = gr_ref[rs, :].astype(F32) * y_ret + gc_ref[rs, :].astype(F32) * y_conv
        x2_ref[rs, :] = x_ref[rs, :] + jnp.dot(m.astype(BF16), wout_ref[...],
                                               preferred_element_type=F32)


def _merge(x2d, o4, sg4, csum_tm, gates, ln_w, ln_b, w_ret_o, w_conv_o, b_conv_o, w_out,
           cast_weights, tm=512):
    m, d = x2d.shape
    _, h, seq, dv = o4.shape
    vw = h * dv
    sblocks = seq // tm
    slabs = _slab_specs(cast_weights, m // tm, lambda i: (i, 0))
    return pl.pallas_call(
        functools.partial(_merge_kernel, row_splits=2, n_cast=len(cast_weights)),
        grid=(m // tm,),
        in_specs=[
            pl.BlockSpec((tm, d), lambda i: (i, 0)),
            pl.BlockSpec((1, h, tm, dv), lambda i: (i // sblocks, 0, i % sblocks, 0)),
            pl.BlockSpec((1, h, tm, dv), lambda i: (i // sblocks, 0, i % sblocks, 0)),
            pl.BlockSpec((tm * SUBLANES, LANES), lambda i: (i, 0)),
            pl.BlockSpec((tm, d), lambda i: (i, 0)),
            pl.BlockSpec((tm, d), lambda i: (i, 1)),
            _const_spec((1, d)),
            _const_spec((1, d)),
            _const_spec((vw, d)),
            _const_spec((d, d)),
            _const_spec((1, d)),
            _const_spec((d, d)),
        ] + slabs,
        out_specs=[pl.BlockSpec((tm, d), lambda i: (i, 0))] + slabs,
        out_shape=[jax.ShapeDtypeStruct((m, d), F32)]
        + [jax.ShapeDtypeStruct(w.shape, BF16) for w in cast_weights],
        compiler_params=pltpu.CompilerParams(
            dimension_semantics=("arbitrary",),
            vmem_limit_bytes=VMEM_LIMIT),
        name="merge",
    )(x2d, o4, sg4, csum_tm, gates, gates, ln_w, ln_b, w_ret_o, w_conv_o, b_conv_o, w_out,
      *cast_weights)


def _mlp_kernel(x_ref, n2_ref, w1_ref, w2_ref, nf_ref, out_ref, *, final_norm, ff_chunk,
                row_splits):
    tm = x_ref.shape[0]
    rows = tm // row_splits
    for r0 in range(0, tm, rows):
        x = x_ref[r0:r0 + rows, :]
        h = (_rms_scale(x) * n2_ref[...]).astype(BF16)
        acc = x
        for c0 in range(0, w1_ref.shape[1], ff_chunk):
            a = jnp.dot(h, w1_ref[:, c0:c0 + ff_chunk], preferred_element_type=F32)
            a = jnp.square(jnp.maximum(a, 0.0)).astype(BF16)
            acc = acc + jnp.dot(a, w2_ref[c0:c0 + ff_chunk, :], preferred_element_type=F32)
        if final_norm:
            acc = _rms_scale(acc) * nf_ref[...]
        out_ref[r0:r0 + rows, :] = acc


def _mlp(x2d, n2, w1, w2, nf, final_norm, tm=1024):
    m, d = x2d.shape
    dff = w1.shape[1]
    return pl.pallas_call(
        functools.partial(_mlp_kernel, final_norm=final_norm, ff_chunk=1024, row_splits=4),
        grid=(m // tm,),
        in_specs=[
            pl.BlockSpec((tm, d), lambda i: (i, 0)),
            _const_spec((1, d)),
            _const_spec((d, dff)),
            _const_spec((dff, d)),
            _const_spec((1, d)),
        ],
        out_specs=pl.BlockSpec((tm, d), lambda i: (i, 0)),
        out_shape=jax.ShapeDtypeStruct((m, d), F32),
        compiler_params=pltpu.CompilerParams(
            dimension_semantics=("arbitrary",),
            vmem_limit_bytes=VMEM_LIMIT),
        name="mlp",
    )(x2d, n2, w1, w2, nf)


def _rotary_tables(seq, block=64):
    half = RET_QK_DIM // 2
    lane = jnp.arange(RET_QK_DIM)
    inv_freq = ROPE_BASE ** (-(2 * (lane % half)).astype(F32) / RET_QK_DIM)
    sign = jnp.where(lane < half, -1.0, 1.0).astype(F32)
    ang_hi = (jnp.arange(seq // block, dtype=F32) * block)[:, None] * inv_freq[None, :]
    ang_lo = jnp.arange(block, dtype=F32)[:, None] * inv_freq[None, :]
    ch, sh = jnp.cos(ang_hi)[:, None, :], jnp.sin(ang_hi)[:, None, :]
    cl, sl = jnp.cos(ang_lo)[None, :, :], jnp.sin(ang_lo)[None, :, :]
    cos2 = (ch * cl - sh * sl).reshape(seq, RET_QK_DIM)
    sin2 = ((sh * cl + ch * sl) * sign).reshape(seq, RET_QK_DIM)
    return cos2, sin2


def kernel(x, norm1_w, w_in, ret_decay_raw, ret_gn_w, w_ret_o, b_glu, conv_w, conv_b, conv_ln_w,
           conv_ln_b, w_conv_o, b_conv_o, w_out, norm2_w, w_mlp1, w_mlp2, norm_f_w):
    b, s, d = x.shape
    depth = w_in.shape[0]

    cos2, sin2 = _rotary_tables(s)

    x2d = x.reshape(b * s, d)
    for l in range(depth):
        q4, kt4, v4, sg4, gates, u = _in_proj(
            x2d, norm1_w[l][None, :], w_in[l], cos2, sin2, b_glu[l][None, :],
            ret_gn_w[l][None, :], b, s)
        raw_b = jnp.broadcast_to(ret_decay_raw[l].T[:, :, None], (RET_HEADS, 2, RET_V_DIM))
        o4 = _retention(q4, kt4, v4, raw_b)
        csum, wro_b, wco_b, wout_b = _conv_sums(
            u, conv_w[l], conv_b[l], (w_ret_o[l], w_conv_o[l], w_out[l]), b, s)
        x2d, w1_b, w2_b = _merge(
            x2d, o4, sg4, csum, gates, conv_ln_w[l][None, :], conv_ln_b[l][None, :],
            wro_b, wco_b, b_conv_o[l][None, :], wout_b, (w_mlp1[l], w_mlp2[l]))
        x2d = _mlp(x2d, norm2_w[l][None, :], w1_b, w2_b, norm_f_w[None, :],
                   final_norm=(l == depth - 1))
    return x2d.reshape(b, s, d)
```

```python
import functools

import jax
import jax.numpy as jnp
from jax import lax
from jax.experimental import pallas as pl
from jax.experimental.pallas import tpu as pltpu

F32 = jnp.float32
BF16 = jnp.bfloat16

RET_HEADS = 8
RET_QK_DIM = 128
RET_V_DIM = 256
CONV_WIDTH = 31
CONV_HALF = CONV_WIDTH // 2
ROPE_BASE = 10000.0
EPS = 1e-6
GN_EPS = 1e-5

LANES = 128
SUBLANES = 8
BF16_ROWS = 16
HALO = 16
RET_CHUNK = 256
VMEM_LIMIT = 56 * 1024 * 1024


def _sigmoid(x):
    return 0.5 * jnp.tanh(0.5 * x) + 0.5


def _rms_scale(x):
    return x * lax.rsqrt(jnp.mean(x * x, axis=-1, keepdims=True) + EPS)


def _const_spec(shape):
    return pl.BlockSpec(shape, lambda i: (0,) * len(shape), pipeline_mode=pl.Buffered(1))


def _slab_specs(weights, nsteps, index_map):
    specs = []
    for w in weights:
        rows = w.shape[0] // nsteps
        assert rows * nsteps == w.shape[0] and rows % BF16_ROWS == 0
        specs.append(pl.BlockSpec((rows, w.shape[1]), index_map))
    return specs


def _cast_weight_once(w_hbm, w_vmem, stage_ref, sem_ref):
    nslots, chunk_rows = stage_ref.shape[0], stage_ref.shape[1]
    nchunks = w_hbm.shape[0] // chunk_rows

    def copy(c):
        slot = c % nslots
        return pltpu.make_async_copy(
            w_hbm.at[pl.ds(c * chunk_rows, chunk_rows), :], stage_ref.at[slot], sem_ref.at[slot])

    for c in range(min(nslots - 1, nchunks)):
        copy(c).start()
    for c in range(nchunks):
        if c + nslots - 1 < nchunks:
            copy(c + nslots - 1).start()
        copy(c).wait()
        w_vmem[c * chunk_rows:(c + 1) * chunk_rows, :] = stage_ref[c % nslots].astype(BF16)


def _in_proj_kernel(x_ref, n1_ref, w_hbm, cos_ref, sin_ref, bglu_ref, gnw_ref,
                    q_ref, kt_ref, v_ref, sg_ref, gates_ref, u_ref, w_ref, stage_ref, sem_ref,
                    *, k_scale, nw, row_splits):
    @pl.when(pl.program_id(0) == 0)
    def _():
        _cast_weight_once(w_hbm, w_ref, stage_ref, sem_ref)

    tm = x_ref.shape[0]
    rows = tm // row_splits
    hv = nw // RET_V_DIM
    for r0 in range(0, tm, rows):
        rs = slice(r0, r0 + rows)
        h = (_rms_scale(x_ref[rs, :]) * n1_ref[...]).astype(BF16)

        def dots(group, h=h):
            lo = 2 * group * nw
            a = jnp.dot(h, w_ref[:, lo:lo + nw], preferred_element_type=F32)
            b = jnp.dot(h, w_ref[:, lo + nw:lo + 2 * nw], preferred_element_type=F32)
            return a, b

        a, b = dots(0)
        c = cos_ref[rs, :]
        s = sin_ref[rs, :]
        for hh in range(nw // RET_QK_DIM):
            t = a[:, hh * RET_QK_DIM:(hh + 1) * RET_QK_DIM]
            rot = pltpu.roll(t, RET_QK_DIM // 2, axis=1)
            q_ref[0, hh, rs, :] = (t * c + rot * s).astype(BF16)
        ck = c * k_scale
        sk = s * k_scale
        for hh in range(nw // RET_QK_DIM):
            t = b[:, hh * RET_QK_DIM:(hh + 1) * RET_QK_DIM]
            rot = pltpu.roll(t, RET_QK_DIM // 2, axis=1)
            kt_ref[0, hh, :, rs] = (t * ck + rot * sk).T.astype(BF16)

        for half, r in enumerate(dots(2)):
            y = (r * _sigmoid(r) * gnw_ref[:, half * nw:(half + 1) * nw]).astype(BF16)
            for hh in range(hv):
                sg_ref[0, half * hv + hh, rs, :] = y[:, hh * RET_V_DIM:(hh + 1) * RET_V_DIM]

        a, b = dots(3)
        u = (a + bglu_ref[:, :nw]) * _sigmoid(b + bglu_ref[:, nw:])
        for g in range(nw // LANES):
            u_ref[pl.ds(r0 * SUBLANES + g, rows, stride=SUBLANES), :] = (
                u[:, g * LANES:(g + 1) * LANES])

        a, b = dots(4)
        gates_ref[rs, :nw] = _sigmoid(a).astype(BF16)
        gates_ref[rs, nw:] = _sigmoid(b).astype(BF16)

        for half, r in enumerate(dots(1)):
            y = r.astype(BF16)
            for hh in range(hv):
                v_ref[0, half * hv + hh, rs, :] = y[:, hh * RET_V_DIM:(hh + 1) * RET_V_DIM]


def _in_proj(x2d, n1, w_in, cos2, sin2, b_glu, gnw, batch, seq, tm=512):
    m, d = x2d.shape
    sblocks = seq // tm
    nw = 1024
    h = RET_HEADS
    assert nw // LANES == SUBLANES and w_in.shape[1] == 10 * nw
    assert nw == h * RET_QK_DIM and 2 * nw == h * RET_V_DIM

    def head_spec(width):
        return pl.BlockSpec((1, h, tm, width), lambda i: (i // sblocks, 0, i % sblocks, 0))

    return pl.pallas_call(
        functools.partial(_in_proj_kernel, k_scale=RET_QK_DIM ** -0.5, nw=nw, row_splits=2),
        grid=(m // tm,),
        in_specs=[
            pl.BlockSpec((tm, d), lambda i: (i, 0)),
            _const_spec((1, d)),
            pl.BlockSpec(memory_space=pl.ANY),
            pl.BlockSpec((tm, RET_QK_DIM), lambda i: (i % sblocks, 0)),
            pl.BlockSpec((tm, RET_QK_DIM), lambda i: (i % sblocks, 0)),
            _const_spec((1, 2 * nw)),
            _const_spec((1, 2 * nw)),
        ],
        out_specs=[
            head_spec(RET_QK_DIM),
            pl.BlockSpec((1, h, RET_QK_DIM, tm), lambda i: (i // sblocks, 0, 0, i % sblocks)),
            head_spec(RET_V_DIM),
            head_spec(RET_V_DIM),
            pl.BlockSpec((tm, 2 * nw), lambda i: (i, 0)),
            pl.BlockSpec((tm * SUBLANES, LANES), lambda i: (i, 0)),
        ],
        out_shape=[
            jax.ShapeDtypeStruct((batch, h, seq, RET_QK_DIM), BF16),
            jax.ShapeDtypeStruct((batch, h, RET_QK_DIM, seq), BF16),
            jax.ShapeDtypeStruct((batch, h, seq, RET_V_DIM), BF16),
            jax.ShapeDtypeStruct((batch, h, seq, RET_V_DIM), BF16),
            jax.ShapeDtypeStruct((m, 2 * nw), BF16),
            jax.ShapeDtypeStruct((m * SUBLANES, LANES), F32),
        ],
        scratch_shapes=[
            pltpu.VMEM((d, 10 * nw), BF16),
            pltpu.VMEM((4, 16, 10 * nw), F32),
            pltpu.SemaphoreType.DMA((4,)),
        ],
        compiler_params=pltpu.CompilerParams(
            dimension_semantics=("arbitrary",),
            vmem_limit_bytes=VMEM_LIMIT),
        name="in_proj",
    )(x2d, n1, w_in, cos2, sin2, b_glu, gnw)


def _ret_kernel(raw_ref, q_ref, kt_ref, v_ref, o_ref, rst_ref, d_ref, zeta_ref, xi_ref):
    c = RET_CHUNK
    dk = RET_QK_DIM
    nchunks = q_ref.shape[2] // c
    lg = -jnp.exp(raw_ref[0])
    lgf = lg[0:1, :]
    lgb = lg[1:2, :]

    row = lax.broadcasted_iota(jnp.int32, (c, c), 0)
    col = lax.broadcasted_iota(jnp.int32, (c, c), 1)
    dpos = jnp.maximum(row - col, 0).astype(F32)
    dneg = jnp.maximum(col - row, 0).astype(F32)
    d_ref[...] = jnp.where(row >= col, jnp.exp(lgf * dpos), jnp.exp(lgb * dneg))

    pos = lax.broadcasted_iota(jnp.int32, (c, dk), 0).astype(F32)
    post = lax.broadcasted_iota(jnp.int32, (dk, c), 1).astype(F32)
    lgf_k = lgf[:, :dk]
    lgb_k = lgb[:, :dk]
    zeta_ref[0] = jnp.exp(lgf * (c - 1.0 - post)).astype(BF16)
    zeta_ref[1] = jnp.exp(lgb * post).astype(BF16)
    xi_ref[:, :dk] = jnp.exp(lgf_k * (pos + 1.0)).astype(BF16)
    xi_ref[:, dk:] = jnp.exp(lgb_k * (c - pos)).astype(BF16)
    g_f = jnp.exp(lgf * c)
    g_b = jnp.exp(lgb * c)

    def state_body(t, carry):
        rf, rb = carry
        nf = t
        nb = nchunks - 1 - t
        rst_ref[nf, 0:dk, :] = rf.astype(BF16)
        rst_ref[nb, dk:, :] = rb.astype(BF16)
        sf = pl.ds(pl.multiple_of(nf * c, c), c)
        sb = pl.ds(pl.multiple_of(nb * c, c), c)
        kzf = kt_ref[0, 0, :, sf] * zeta_ref[0]
        kzb = kt_ref[0, 0, :, sb] * zeta_ref[1]
        uf = jnp.dot(kzf, v_ref[0, 0, sf, :], preferred_element_type=F32)
        ub = jnp.dot(kzb, v_ref[0, 0, sb, :], preferred_element_type=F32)
        return g_f * rf + uf, g_b * rb + ub

    zero = jnp.zeros((dk, RET_V_DIM), F32)
    lax.fori_loop(0, nchunks, state_body, (zero, zero), unroll=True)

    def out_body(n, _):
        sl = pl.ds(pl.multiple_of(n * c, c), c)
        qc = q_ref[0, 0, sl, :]
        s = jnp.dot(qc, kt_ref[0, 0, :, sl], preferred_element_type=F32)
        p = (s * d_ref[...]).astype(BF16)
        qx = jnp.concatenate([qc, qc], axis=1) * xi_ref[...]
        o = (jnp.dot(p, v_ref[0, 0, sl, :], preferred_element_type=F32)
             + jnp.dot(qx, rst_ref[n], preferred_element_type=F32))
        mu = jnp.mean(o, axis=-1, keepdims=True)
        oc = o - mu
        var = jnp.mean(oc * oc, axis=-1, keepdims=True)
        o_ref[0, 0, sl, :] = (oc * lax.rsqrt(var + GN_EPS)).astype(BF16)
        return 0

    lax.fori_loop(0, nchunks, out_body, 0, unroll=True)


def _retention(q4, kt4, v4, raw_b):
    b, h, s, _ = q4.shape
    nchunks = s // RET_CHUNK

    def head_spec(width):
        return pl.BlockSpec((1, 1, s, width), lambda bi, hi: (bi, hi, 0, 0))

    return pl.pallas_call(
        _ret_kernel,
        grid=(b, h),
        in_specs=[
            pl.BlockSpec((1, 2, RET_V_DIM), lambda bi, hi: (hi, 0, 0)),
            head_spec(RET_QK_DIM),
            pl.BlockSpec((1, 1, RET_QK_DIM, s), lambda bi, hi: (bi, hi, 0, 0)),
            head_spec(RET_V_DIM),
        ],
        out_specs=head_spec(RET_V_DIM),
        out_shape=jax.ShapeDtypeStruct((b, h, s, RET_V_DIM), BF16),
        scratch_shapes=[
            pltpu.VMEM((nchunks, 2 * RET_QK_DIM, RET_V_DIM), BF16),
            pltpu.VMEM((RET_CHUNK, RET_CHUNK), F32),
            pltpu.VMEM((2, RET_QK_DIM, RET_CHUNK), BF16),
            pltpu.VMEM((RET_CHUNK, 2 * RET_QK_DIM), BF16),
        ],
        compiler_params=pltpu.CompilerParams(
            dimension_semantics=("arbitrary", "arbitrary"),
            vmem_limit_bytes=VMEM_LIMIT),
        name="retention",
    )(raw_b, q4, kt4, v4)


def _conv_kernel(prev_ref, main_ref, next_ref, cw_ref, cb_ref, *rest, tblk, n_cast):
    cast_in, out_ref, cast_out, edge_ref = (rest[:n_cast], rest[n_cast], rest[n_cast + 1:-1],
                                             rest[-1])
    for src, dst in zip(cast_in, cast_out):
        dst[...] = src[...].astype(BF16)

    i = pl.program_id(1)
    last = pl.num_programs(1) - 1
    sl = SUBLANES
    ts = main_ref.shape[0] // sl
    span = tblk + CONV_WIDTH - 1
    nblocks = ts // tblk
    assert span - tblk <= 2 * HALO and nblocks >= 3

    w = [cw_ref[k * sl:(k + 1) * sl, :] for k in range(CONV_WIDTH)]
    bias = cb_ref[...]

    def conv_block(src_ref, src_tok0, out_tok0):
        accs = [bias] * tblk
        for m in range(span):
            start = (src_tok0 + m) * sl
            if not isinstance(start, int):
                start = pl.multiple_of(start, sl)
            tok = src_ref[pl.ds(start, sl), :]
            for tt in range(tblk):
                k = m - tt
                if 0 <= k < CONV_WIDTH:
                    accs[tt] = accs[tt] + tok * w[k]
        for tt in range(tblk):
            o = (out_tok0 + tt) * sl
            if not isinstance(o, int):
                o = pl.multiple_of(o, sl)
            out_ref[pl.ds(o, sl), :] = accs[tt]

    edge_ref[0:HALO * sl, :] = jnp.where(i > 0, prev_ref[...], 0.0)
    edge_ref[HALO * sl:(HALO + span) * sl, :] = main_ref[0:span * sl, :]
    conv_block(edge_ref, HALO - CONV_HALF, 0)
    e1 = HALO + span
    edge_ref[e1 * sl:(e1 + span) * sl, :] = main_ref[(ts - span) * sl:, :]
    edge_ref[(e1 + span) * sl:, :] = jnp.where(i < last, next_ref[...], 0.0)
    conv_block(edge_ref, e1 + span - tblk - CONV_HALF, ts - tblk)

    def conv_body(bi, _):
        t0 = bi * tblk
        conv_block(main_ref, t0 - CONV_HALF, t0)
        return 0

    lax.fori_loop(1, nblocks - 1, conv_body, 0)


def _conv_sums(u_tm, conv_w, conv_b, cast_weights, batch, seq, ts=1024, tblk=16):
    sl = SUBLANES
    nblk = seq // ts
    hb = ts // HALO
    nhalo = batch * seq // HALO
    cw = conv_w.reshape(CONV_WIDTH * sl, LANES)
    cb = conv_b.reshape(sl, LANES)
    slabs = _slab_specs(cast_weights, batch * nblk, lambda bi, i: (bi * nblk + i, 0))
    return pl.pallas_call(
        functools.partial(_conv_kernel, tblk=tblk, n_cast=len(cast_weights)),
        grid=(batch, nblk),
        in_specs=[
            pl.BlockSpec((HALO * sl, LANES),
                         lambda bi, i: (jnp.maximum((bi * nblk + i) * hb - 1, 0), 0)),
            pl.BlockSpec((ts * sl, LANES), lambda bi, i: (bi * nblk + i, 0)),
            pl.BlockSpec((HALO * sl, LANES),
                         lambda bi, i: (jnp.minimum((bi * nblk + i + 1) * hb, nhalo - 1), 0)),
            pl.BlockSpec((CONV_WIDTH * sl, LANES), lambda bi, i: (0, 0)),
            pl.BlockSpec((sl, LANES), lambda bi, i: (0, 0)),
        ] + slabs,
        out_specs=[pl.BlockSpec((ts * sl, LANES), lambda bi, i: (bi * nblk + i, 0))] + slabs,
        out_shape=[jax.ShapeDtypeStruct(u_tm.shape, F32)]
        + [jax.ShapeDtypeStruct(w.shape, BF16) for w in cast_weights],
        scratch_shapes=[pltpu.VMEM((2 * (HALO + tblk + CONV_WIDTH - 1) * sl, LANES), F32)],
        compiler_params=pltpu.CompilerParams(
            dimension_semantics=("arbitrary", "arbitrary"),
            vmem_limit_bytes=VMEM_LIMIT),
        name="conv_sums",
    )(u_tm, u_tm, u_tm, cw, cb, *cast_weights)


def _merge_mlp_kernel(x_ref, o_ref, sg_ref, cs_ref, gr_ref, gc_ref, lnw_ref, lnb_ref, wro_ref,
                      wco_ref, bco_ref, wout_ref, n2_ref, w1_ref, w2_ref, nf_ref, out_ref,
                      *, row_splits, final_norm, ff_chunk):
    tm = x_ref.shape[0]
    rows = tm // row_splits
    sl = SUBLANES
    for r0 in range(0, tm, rows):
        rs = slice(r0, r0 + rows)
        og = jnp.concatenate([o_ref[0, hh, rs, :] * sg_ref[0, hh, rs, :]
                              for hh in range(o_ref.shape[1])], axis=1)
        y_ret = jnp.dot(og, wro_ref[...], preferred_element_type=F32)
        a = jnp.concatenate(
            [cs_ref[pl.ds(r0 * sl + g, rows, stride=sl), :] for g in range(sl)], axis=1)
        mu = jnp.mean(a, axis=-1, keepdims=True)
        ac = a - mu
        var = jnp.mean(ac * ac, axis=-1, keepdims=True)
        y = ac * lax.rsqrt(var + GN_EPS) * lnw_ref[...] + lnb_ref[...]
        uc = (y * _sigmoid(y)).astype(BF16)
        y_conv = jnp.dot(uc, wco_ref[...], preferred_element_type=F32) + bco_ref[...]
        m = gr_ref[rs, :].astype(F32) * y_ret + gc_ref[rs, :].astype(F32) * y_conv
        x = x_ref[rs, :] + jnp.dot(m.astype(BF16), wout_ref[...], preferred_element_type=F32)
        h = (_rms_scale(x) * n2_ref[...]).astype(BF16)
        acc = x
        for c0 in range(0, w1_ref.shape[1], ff_chunk):
            t = jnp.dot(h, w1_ref[:, c0:c0 + ff_chunk], preferred_element_type=F32)
            t = jnp.square(jnp.maximum(t, 0.0)).astype(BF16)
            acc = acc + jnp.dot(t, w2_ref[c0:c0 + ff_chunk, :], preferred_element_type=F32)
        if final_norm:
            acc = _rms_scale(acc) * nf_ref[...]
        out_ref[rs, :] = acc


def _merge_mlp(x2d, o4, sg4, csum_tm, gates, ln_w, ln_b, w_ret_o, w_conv_o, b_conv_o, w_out,
               n2, w1, w2, nf, final_norm, tm=512):
    m, d = x2d.shape
    _, h, seq, dv = o4.shape
    vw = h * dv
    dff = w1.shape[1]
    sblocks = seq // tm
    return pl.pallas_call(
        functools.partial(_merge_mlp_kernel, row_splits=2, final_norm=final_norm, ff_chunk=1024),
        grid=(m // tm,),
        in_specs=[
            pl.BlockSpec((tm, d), lambda i: (i, 0)),
            pl.BlockSpec((1, h, tm, dv), lambda i: (i // sblocks, 0, i % sblocks, 0)),
            pl.BlockSpec((1, h, tm, dv), lambda i: (i // sblocks, 0, i % sblocks, 0)),
            pl.BlockSpec((tm * SUBLANES, LANES), lambda i: (i, 0)),
            pl.BlockSpec((tm, d), lambda i: (i, 0)),
            pl.BlockSpec((tm, d), lambda i: (i, 1)),
            _const_spec((1, d)),
            _const_spec((1, d)),
            _const_spec((vw, d)),
            _const_spec((d, d)),
            _const_spec((1, d)),
            _const_spec((d, d)),
            _const_spec((1, d)),
            _const_spec((d, dff)),
            _const_spec((dff, d)),
            _const_spec((1, d)),
        ],
        out_specs=pl.BlockSpec((tm, d), lambda i: (i, 0)),
        out_shape=jax.ShapeDtypeStruct((m, d), F32),
        compiler_params=pltpu.CompilerParams(
            dimension_semantics=("arbitrary",),
            vmem_limit_bytes=VMEM_LIMIT),
        name="merge_mlp",
    )(x2d, o4, sg4, csum_tm, gates, gates, ln_w, ln_b, w_ret_o, w_conv_o, b_conv_o, w_out,
      n2, w1, w2, nf)


def _rotary_tables(seq, block=64):
    half = RET_QK_DIM // 2
    lane = jnp.arange(RET_QK_DIM)
    inv_freq = ROPE_BASE ** (-(2 * (lane % half)).astype(F32) / RET_QK_DIM)
    sign = jnp.where(lane < half, -1.0, 1.0).astype(F32)
    ang_hi = (jnp.arange(seq // block, dtype=F32) * block)[:, None] * inv_freq[None, :]
    ang_lo = jnp.arange(block, dtype=F32)[:, None] * inv_freq[None, :]
    ch, sh = jnp.cos(ang_hi)[:, None, :], jnp.sin(ang_hi)[:, None, :]
    cl, sl = jnp.cos(ang_lo)[None, :, :], jnp.sin(ang_lo)[None, :, :]
    cos2 = (ch * cl - sh * sl).reshape(seq, RET_QK_DIM)
    sin2 = ((sh * cl + ch * sl) * sign).reshape(seq, RET_QK_DIM)
    return cos2, sin2


def kernel(x, norm1_w, w_in, ret_decay_raw, ret_gn_w, w_ret_o, b_glu, conv_w, conv_b, conv_ln_w,
           conv_ln_b, w_conv_o, b_conv_o, w_out, norm2_w, w_mlp1, w_mlp2, norm_f_w):
    b, s, d = x.shape
    depth = w_in.shape[0]

    cos2, sin2 = _rotary_tables(s)

    x2d = x.reshape(b * s, d)
    for l in range(depth):
        q4, kt4, v4, sg4, gates, u = _in_proj(
            x2d, norm1_w[l][None, :], w_in[l], cos2, sin2, b_glu[l][None, :],
            ret_gn_w[l][None, :], b, s)
        raw_b = jnp.broadcast_to(ret_decay_raw[l].T[:, :, None], (RET_HEADS, 2, RET_V_DIM))
        o4 = _retention(q4, kt4, v4, raw_b)
        csum, wro_b, wco_b, wout_b, w1_b, w2_b = _conv_sums(
            u, conv_w[l], conv_b[l], (w_ret_o[l], w_conv_o[l], w_out[l], w_mlp1[l], w_mlp2[l]),
            b, s)
        x2d = _merge_mlp(x2d, o4, sg4, csum, gates, conv_ln_w[l][None, :], conv_ln_b[l][None, :],
                         wro_b, wco_b, b_conv_o[l][None, :], wout_b, norm2_w[l][None, :],
                         w1_b, w2_b, norm_f_w[None, :], final_norm=(l == depth - 1))
    return x2d.reshape(b, s, d)
```

```python
import functools

import jax
import jax.numpy as jnp
from jax import lax
from jax.experimental import pallas as pl
from jax.experimental.pallas import tpu as pltpu

F32 = jnp.float32
BF16 = jnp.bfloat16

RET_HEADS = 8
RET_QK_DIM = 128
RET_V_DIM = 256
CONV_WIDTH = 31
CONV_HALF = CONV_WIDTH // 2
ROPE_BASE = 10000.0
EPS = 1e-6
GN_EPS = 1e-5

LANES = 128
SUBLANES = 8
BF16_ROWS = 16
HALO = 16
RET_CHUNK = 256
VMEM_LIMIT = 56 * 1024 * 1024


def _sigmoid(x):
    return 0.5 * jnp.tanh(0.5 * x) + 0.5


def _rms_scale(x):
    return x * lax.rsqrt(jnp.mean(x * x, axis=-1, keepdims=True) + EPS)


def _const_spec(shape):
    return pl.BlockSpec(shape, lambda i: (0,) * len(shape), pipeline_mode=pl.Buffered(1))


def _slab_specs(weights, nsteps, index_map):
    specs = []
    for w in weights:
        rows = w.shape[0] // nsteps
        assert rows * nsteps == w.shape[0] and rows % BF16_ROWS == 0
        specs.append(pl.BlockSpec((rows, w.shape[1]), index_map))
    return specs


def _cast_weight_once(w_hbm, w_vmem, stage_ref, sem_ref):
    nslots, chunk_rows = stage_ref.shape[0], stage_ref.shape[1]
    nchunks = w_hbm.shape[0] // chunk_rows

    def copy(c):
        slot = c % nslots
        return pltpu.make_async_copy(
            w_hbm.at[pl.ds(c * chunk_rows, chunk_rows), :], stage_ref.at[slot], sem_ref.at[slot])

    for c in range(min(nslots - 1, nchunks)):
        copy(c).start()
    for c in range(nchunks):
        if c + nslots - 1 < nchunks:
            copy(c + nslots - 1).start()
        copy(c).wait()
        w_vmem[c * chunk_rows:(c + 1) * chunk_rows, :] = stage_ref[c % nslots].astype(BF16)


def _in_proj_kernel(x_ref, n1_ref, w_hbm, cos_ref, sin_ref, bglu_ref, gnw_ref,
                    q_ref, kt_ref, v_ref, sg_ref, gates_ref, u_ref, w_ref, stage_ref, sem_ref,
                    *, k_scale, nw, row_splits):
    @pl.when(pl.program_id(0) == 0)
    def _():
        _cast_weight_once(w_hbm, w_ref, stage_ref, sem_ref)

    tm = x_ref.shape[0]
    rows = tm // row_splits
    hv = nw // RET_V_DIM
    for r0 in range(0, tm, rows):
        rs = slice(r0, r0 + rows)
        h = (_rms_scale(x_ref[rs, :]) * n1_ref[...]).astype(BF16)

        def dots(group, h=h):
            lo = 2 * group * nw
            a = jnp.dot(h, w_ref[:, lo:lo + nw], preferred_element_type=F32)
            b = jnp.dot(h, w_ref[:, lo + nw:lo + 2 * nw], preferred_element_type=F32)
            return a, b

        a, b = dots(0)
        c = cos_ref[rs, :]
        s = sin_ref[rs, :]
        for hh in range(nw // RET_QK_DIM):
            t = a[:, hh * RET_QK_DIM:(hh + 1) * RET_QK_DIM]
            rot = pltpu.roll(t, RET_QK_DIM // 2, axis=1)
            q_ref[0, hh, rs, :] = (t * c + rot * s).astype(BF16)
        ck = c * k_scale
        sk = s * k_scale
        for hh in range(nw // RET_QK_DIM):
            t = b[:, hh * RET_QK_DIM:(hh + 1) * RET_QK_DIM]
            rot = pltpu.roll(t, RET_QK_DIM // 2, axis=1)
            kt_ref[0, hh, :, rs] = (t * ck + rot * sk).T.astype(BF16)

        for half, r in enumerate(dots(2)):
            y = (r * _sigmoid(r) * gnw_ref[:, half * nw:(half + 1) * nw]).astype(BF16)
            for hh in range(hv):
                sg_ref[0, half * hv + hh, rs, :] = y[:, hh * RET_V_DIM:(hh + 1) * RET_V_DIM]

        a, b = dots(3)
        u = (a + bglu_ref[:, :nw]) * _sigmoid(b + bglu_ref[:, nw:])
        for g in range(nw // LANES):
            u_ref[pl.ds(r0 * SUBLANES + g, rows, stride=SUBLANES), :] = (
                u[:, g * LANES:(g + 1) * LANES])

        a, b = dots(4)
        gates_ref[rs, :nw] = _sigmoid(a).astype(BF16)
        gates_ref[rs, nw:] = _sigmoid(b).astype(BF16)

        for half, r in enumerate(dots(1)):
            y = r.astype(BF16)
            for hh in range(hv):
                v_ref[0, half * hv + hh, rs, :] = y[:, hh * RET_V_DIM:(hh + 1) * RET_V_DIM]


def _in_proj(x2d, n1, w_in, cos2, sin2, b_glu, gnw, batch, seq, tm=512):
    m, d = x2d.shape
    sblocks = seq // tm
    nw = 1024
    h = RET_HEADS
    assert nw // LANES == SUBLANES and w_in.shape[1] == 10 * nw
    assert nw == h * RET_QK_DIM and 2 * nw == h * RET_V_DIM

    def head_spec(width):
        return pl.BlockSpec((1, h, tm, width), lambda i: (i // sblocks, 0, i % sblocks, 0))

    return pl.pallas_call(
        functools.partial(_in_proj_kernel, k_scale=RET_QK_DIM ** -0.5, nw=nw, row_splits=2),
        grid=(m // tm,),
        in_specs=[
            pl.BlockSpec((tm, d), lambda i: (i, 0)),
            _const_spec((1, d)),
            pl.BlockSpec(memory_space=pl.ANY),
            pl.BlockSpec((tm, RET_QK_DIM), lambda i: (i % sblocks, 0)),
            pl.BlockSpec((tm, RET_QK_DIM), lambda i: (i % sblocks, 0)),
            _const_spec((1, 2 * nw)),
            _const_spec((1, 2 * nw)),
        ],
        out_specs=[
            head_spec(RET_QK_DIM),
            pl.BlockSpec((1, h, RET_QK_DIM, tm), lambda i: (i // sblocks, 0, 0, i % sblocks)),
            head_spec(RET_V_DIM),
            head_spec(RET_V_DIM),
            pl.BlockSpec((tm, 2 * nw), lambda i: (i, 0)),
            pl.BlockSpec((tm * SUBLANES, LANES), lambda i: (i, 0)),
        ],
        out_shape=[
            jax.ShapeDtypeStruct((batch, h, seq, RET_QK_DIM), BF16),
            jax.ShapeDtypeStruct((batch, h, RET_QK_DIM, seq), BF16),
            jax.ShapeDtypeStruct((batch, h, seq, RET_V_DIM), BF16),
            jax.ShapeDtypeStruct((batch, h, seq, RET_V_DIM), BF16),
            jax.ShapeDtypeStruct((m, 2 * nw), BF16),
            jax.ShapeDtypeStruct((m * SUBLANES, LANES), F32),
        ],
        scratch_shapes=[
            pltpu.VMEM((d, 10 * nw), BF16),
            pltpu.VMEM((4, 16, 10 * nw), F32),
            pltpu.SemaphoreType.DMA((4,)),
        ],
        compiler_params=pltpu.CompilerParams(
            dimension_semantics=("arbitrary",),
            vmem_limit_bytes=VMEM_LIMIT),
        name="in_proj",
    )(x2d, n1, w_in, cos2, sin2, b_glu, gnw)


def _ret_kernel(raw_ref, q_ref, kt_ref, v_ref, o_ref, rst_ref, d_ref, zeta_ref, xi_ref):
    c = RET_CHUNK
    dk = RET_QK_DIM
    nchunks = q_ref.shape[2] // c
    lg = -jnp.exp(raw_ref[0])
    lgf = lg[0:1, :]
    lgb = lg[1:2, :]

    row = lax.broadcasted_iota(jnp.int32, (c, c), 0)
    col = lax.broadcasted_iota(jnp.int32, (c, c), 1)
    dpos = jnp.maximum(row - col, 0).astype(F32)
    dneg = jnp.maximum(col - row, 0).astype(F32)
    d_ref[...] = jnp.where(row >= col, jnp.exp(lgf * dpos), jnp.exp(lgb * dneg))

    pos = lax.broadcasted_iota(jnp.int32, (c, dk), 0).astype(F32)
    post = lax.broadcasted_iota(jnp.int32, (dk, c), 1).astype(F32)
    lgf_k = lgf[:, :dk]
    lgb_k = lgb[:, :dk]
    zeta_ref[0] = jnp.exp(lgf * (c - 1.0 - post)).astype(BF16)
    zeta_ref[1] = jnp.exp(lgb * post).astype(BF16)
    xi_ref[:, :dk] = jnp.exp(lgf_k * (pos + 1.0)).astype(BF16)
    xi_ref[:, dk:] = jnp.exp(lgb_k * (c - pos)).astype(BF16)
    g_f = jnp.exp(lgf * c)
    g_b = jnp.exp(lgb * c)

    def state_body(t, carry):
        rf, rb = carry
        nf = t
        nb = nchunks - 1 - t
        rst_ref[nf, 0:dk, :] = rf.astype(BF16)
        rst_ref[nb, dk:, :] = rb.astype(BF16)
        sf = pl.ds(pl.multiple_of(nf * c, c), c)
        sb = pl.ds(pl.multiple_of(nb * c, c), c)
        kzf = kt_ref[0, 0, :, sf] * zeta_ref[0]
        kzb = kt_ref[0, 0, :, sb] * zeta_ref[1]
        uf = jnp.dot(kzf, v_ref[0, 0, sf, :], preferred_element_type=F32)
        ub = jnp.dot(kzb, v_ref[0, 0, sb, :], preferred_element_type=F32)
        return g_f * rf + uf, g_b * rb + ub

    zero = jnp.zeros((dk, RET_V_DIM), F32)
    lax.fori_loop(0, nchunks, state_body, (zero, zero), unroll=True)

    def out_body(n, _):
        sl = pl.ds(pl.multiple_of(n * c, c), c)
        qc = q_ref[0, 0, sl, :]
        s = jnp.dot(qc, kt_ref[0, 0, :, sl], preferred_element_type=F32)
        p = (s * d_ref[...]).astype(BF16)
        qx = jnp.concatenate([qc, qc], axis=1) * xi_ref[...]
        o = (jnp.dot(p, v_ref[0, 0, sl, :], preferred_element_type=F32)
             + jnp.dot(qx, rst_ref[n], preferred_element_type=F32))
        mu = jnp.mean(o, axis=-1, keepdims=True)
        oc = o - mu
        var = jnp.mean(oc * oc, axis=-1, keepdims=True)
        o_ref[0, 0, sl, :] = (oc * lax.rsqrt(var + GN_EPS)).astype(BF16)
        return 0

    lax.fori_loop(0, nchunks, out_body, 0, unroll=True)


def _retention(q4, kt4, v4, raw_b):
    b, h, s, _ = q4.shape
    nchunks = s // RET_CHUNK

    def head_spec(width):
        return pl.BlockSpec((1, 1, s, width), lambda bi, hi: (bi, hi, 0, 0))

    return pl.pallas_call(
        _ret_kernel,
        grid=(b, h),
        in_specs=[
            pl.BlockSpec((1, 2, RET_V_DIM), lambda bi, hi: (hi, 0, 0)),
            head_spec(RET_QK_DIM),
            pl.BlockSpec((1, 1, RET_QK_DIM, s), lambda bi, hi: (bi, hi, 0, 0)),
            head_spec(RET_V_DIM),
        ],
        out_specs=head_spec(RET_V_DIM),
        out_shape=jax.ShapeDtypeStruct((b, h, s, RET_V_DIM), BF16),
        scratch_shapes=[
            pltpu.VMEM((nchunks, 2 * RET_QK_DIM, RET_V_DIM), BF16),
            pltpu.VMEM((RET_CHUNK, RET_CHUNK), F32),
            pltpu.VMEM((2, RET_QK_DIM, RET_CHUNK), BF16),
            pltpu.VMEM((RET_CHUNK, 2 * RET_QK_DIM), BF16),
        ],
        compiler_params=pltpu.CompilerParams(
            dimension_semantics=("arbitrary", "arbitrary"),
            vmem_limit_bytes=VMEM_LIMIT),
        name="retention",
    )(raw_b, q4, kt4, v4)


def _conv_kernel(prev_ref, main_ref, next_ref, cw_ref, cb_ref, *rest, tblk, n_cast):
    cast_in, out_ref, cast_out, edge_ref = (rest[:n_cast], rest[n_cast], rest[n_cast + 1:-1],
                                             rest[-1])
    for src, dst in zip(cast_in, cast_out):
        dst[...] = src[...].astype(BF16)

    i = pl.program_id(1)
    last = pl.num_programs(1) - 1
    sl = SUBLANES
    ts = main_ref.shape[0] // sl
    span = tblk + CONV_WIDTH - 1
    nblocks = ts // tblk
    assert span - tblk <= 2 * HALO and nblocks >= 3

    w = [cw_ref[k * sl:(k + 1) * sl, :] for k in range(CONV_WIDTH)]
    bias = cb_ref[...]

    def conv_block(src_ref, src_tok0, out_tok0):
        accs = [bias] * tblk
        for m in range(span):
            start = (src_tok0 + m) * sl
            if not isinstance(start, int):
                start = pl.multiple_of(start, sl)
            tok = src_ref[pl.ds(start, sl), :]
            for tt in range(tblk):
                k = m - tt
                if 0 <= k < CONV_WIDTH:
                    accs[tt] = accs[tt] + tok * w[k]
        for tt in range(tblk):
            o = (out_tok0 + tt) * sl
            if not isinstance(o, int):
                o = pl.multiple_of(o, sl)
            out_ref[pl.ds(o, sl), :] = accs[tt]

    edge_ref[0:HALO * sl, :] = jnp.where(i > 0, prev_ref[...], 0.0)
    edge_ref[HALO * sl:(HALO + span) * sl, :] = main_ref[0:span * sl, :]
    conv_block(edge_ref, HALO - CONV_HALF, 0)
    e1 = HALO + span
    edge_ref[e1 * sl:(e1 + span) * sl, :] = main_ref[(ts - span) * sl:, :]
    edge_ref[(e1 + span) * sl:, :] = jnp.where(i < last, next_ref[...], 0.0)
    conv_block(edge_ref, e1 + span - tblk - CONV_HALF, ts - tblk)

    def conv_body(bi, _):
        t0 = bi * tblk
        conv_block(main_ref, t0 - CONV_HALF, t0)
        return 0

    lax.fori_loop(1, nblocks - 1, conv_body, 0)


def _conv_sums(u_tm, conv_w, conv_b, cast_weights, batch, seq, ts=2048, tblk=16):
    sl = SUBLANES
    nblk = seq // ts
    hb = ts // HALO
    nhalo = batch * seq // HALO
    cw = conv_w.reshape(CONV_WIDTH * sl, LANES)
    cb = conv_b.reshape(sl, LANES)
    slabs = _slab_specs(cast_weights, batch * nblk, lambda bi, i: (bi * nblk + i, 0))
    return pl.pallas_call(
        functools.partial(_conv_kernel, tblk=tblk, n_cast=len(cast_weights)),
        grid=(batch, nblk),
        in_specs=[
            pl.BlockSpec((HALO * sl, LANES),
                         lambda bi, i: (jnp.maximum((bi * nblk + i) * hb - 1, 0), 0)),
            pl.BlockSpec((ts * sl, LANES), lambda bi, i: (bi * nblk + i, 0)),
            pl.BlockSpec((HALO * sl, LANES),
                         lambda bi, i: (jnp.minimum((bi * nblk + i + 1) * hb, nhalo - 1), 0)),
            pl.BlockSpec((CONV_WIDTH * sl, LANES), lambda bi, i: (0, 0)),
            pl.BlockSpec((sl, LANES), lambda bi, i: (0, 0)),
        ] + slabs,
        out_specs=[pl.BlockSpec((ts * sl, LANES), lambda bi, i: (bi * nblk + i, 0))] + slabs,
        out_shape=[jax.ShapeDtypeStruct(u_tm.shape, F32)]
        + [jax.ShapeDtypeStruct(w.shape, BF16) for w in cast_weights],
        scratch_shapes=[pltpu.VMEM((2 * (HALO + tblk + CONV_WIDTH - 1) * sl, LANES), F32)],
        compiler_params=pltpu.CompilerParams(
            dimension_semantics=("arbitrary", "arbitrary"),
            vmem_limit_bytes=VMEM_LIMIT),
        name="conv_sums",
    )(u_tm, u_tm, u_tm, cw, cb, *cast_weights)


def _merge_kernel(x_ref, o_ref, sg_ref, cs_ref, gr_ref, gc_ref, lnw_ref, lnb_ref, wro_ref, wco_ref,
                  bco_ref, wout_ref, x2_ref, *, row_splits):
    tm = x_ref.shape[0]
    rows = tm // row_splits
    sl = SUBLANES
    for r0 in range(0, tm, rows):
        rs = slice(r0, r0 + rows)
        og = jnp.concatenate([o_ref[0, hh, rs, :] * sg_ref[0, hh, rs, :]
                              for hh in range(o_ref.shape[1])], axis=1)
        y_ret = jnp.dot(og, wro_ref[...], preferred_element_type=F32)
        a = jnp.concatenate(
            [cs_ref[pl.ds(r0 * sl + g, rows, stride=sl), :] for g in range(sl)], axis=1)
        mu = jnp.mean(a, axis=-1, keepdims=True)
        ac = a - mu
        var = jnp.mean(ac * ac, axis=-1, keepdims=True)
        y = ac * lax.rsqrt(var + GN_EPS) * lnw_ref[...] + lnb_ref[...]
        uc = (y * _sigmoid(y)).astype(BF16)
        y_conv = jnp.dot(uc, wco_ref[...], preferred_element_type=F32) + bco_ref[...]
        m = gr_ref[rs, :].astype(F32) * y_ret + gc_ref[rs, :].astype(F32) * y_conv
        x2_ref[rs, :] = x_ref[rs, :] + jnp.dot(m.astype(BF16), wout_ref[...],
                                               preferred_element_type=F32)


def _merge(x2d, o4, sg4, csum_tm, gates, ln_w, ln_b, w_ret_o, w_conv_o, b_conv_o, w_out,
           tm=512):
    m, d = x2d.shape
    _, h, seq, dv = o4.shape
    vw = h * dv
    sblocks = seq // tm
    return pl.pallas_call(
        functools.partial(_merge_kernel, row_splits=2),
        grid=(m // tm,),
        in_specs=[
            pl.BlockSpec((tm, d), lambda i: (i, 0)),
            pl.BlockSpec((1, h, tm, dv), lambda i: (i // sblocks, 0, i % sblocks, 0)),
            pl.BlockSpec((1, h, tm, dv), lambda i: (i // sblocks, 0, i % sblocks, 0)),
            pl.BlockSpec((tm * SUBLANES, LANES), lambda i: (i, 0)),
            pl.BlockSpec((tm, d), lambda i: (i, 0)),
            pl.BlockSpec((tm, d), lambda i: (i, 1)),
            _const_spec((1, d)),
            _const_spec((1, d)),
            _const_spec((vw, d)),
            _const_spec((d, d)),
            _const_spec((1, d)),
            _const_spec((d, d)),
        ],
        out_specs=pl.BlockSpec((tm, d), lambda i: (i, 0)),
        out_shape=jax.ShapeDtypeStruct((m, d), F32),
        compiler_params=pltpu.CompilerParams(
            dimension_semantics=("arbitrary",),
            vmem_limit_bytes=VMEM_LIMIT),
        name="merge",
    )(x2d, o4, sg4, csum_tm, gates, gates, ln_w, ln_b, w_ret_o, w_conv_o, b_conv_o, w_out)


def _mlp_kernel(x_ref, n2_ref, w1_ref, w2_ref, nf_ref, out_ref, *, final_norm, ff_chunk,
                row_splits):
    tm = x_ref.shape[0]
    rows = tm // row_splits
    for r0 in range(0, tm, rows):
        x = x_ref[r0:r0 + rows, :]
        h = (_rms_scale(x) * n2_ref[...]).astype(BF16)
        acc = x
        for c0 in range(0, w1_ref.shape[1], ff_chunk):
            a = jnp.dot(h, w1_ref[:, c0:c0 + ff_chunk], preferred_element_type=F32)
            a = jnp.square(jnp.maximum(a, 0.0)).astype(BF16)
            acc = acc + jnp.dot(a, w2_ref[c0:c0 + ff_chunk, :], preferred_element_type=F32)
        if final_norm:
            acc = _rms_scale(acc) * nf_ref[...]
        out_ref[r0:r0 + rows, :] = acc


def _mlp(x2d, n2, w1, w2, nf, final_norm, tm=2048):
    m, d = x2d.shape
    dff = w1.shape[1]
    return pl.pallas_call(
        functools.partial(_mlp_kernel, final_norm=final_norm, ff_chunk=1024, row_splits=8),
        grid=(m // tm,),
        in_specs=[
            pl.BlockSpec((tm, d), lambda i: (i, 0)),
            _const_spec((1, d)),
            _const_spec((d, dff)),
            _const_spec((dff, d)),
            _const_spec((1, d)),
        ],
        out_specs=pl.BlockSpec((tm, d), lambda i: (i, 0)),
        out_shape=jax.ShapeDtypeStruct((m, d), F32),
        compiler_params=pltpu.CompilerParams(
            dimension_semantics=("arbitrary",),
            vmem_limit_bytes=VMEM_LIMIT),
        name="mlp",
    )(x2d, n2, w1, w2, nf)


def _rotary_tables(seq, block=64):
    half = RET_QK_DIM // 2
    lane = jnp.arange(RET_QK_DIM)
    inv_freq = ROPE_BASE ** (-(2 * (lane % half)).astype(F32) / RET_QK_DIM)
    sign = jnp.where(lane < half, -1.0, 1.0).astype(F32)
    ang_hi = (jnp.arange(seq // block, dtype=F32) * block)[:, None] * inv_freq[None, :]
    ang_lo = jnp.arange(block, dtype=F32)[:, None] * inv_freq[None, :]
    ch, sh = jnp.cos(ang_hi)[:, None, :], jnp.sin(ang_hi)[:, None, :]
    cl, sl = jnp.cos(ang_lo)[None, :, :], jnp.sin(ang_lo)[None, :, :]
    cos2 = (ch * cl - sh * sl).reshape(seq, RET_QK_DIM)
    sin2 = ((sh * cl + ch * sl) * sign).reshape(seq, RET_QK_DIM)
    return cos2, sin2


def kernel(x, norm1_w, w_in, ret_decay_raw, ret_gn_w, w_ret_o, b_glu, conv_w, conv_b, conv_ln_w,
           conv_ln_b, w_conv_o, b_conv_o, w_out, norm2_w, w_mlp1, w_mlp2, norm_f_w):
    b, s, d = x.shape
    depth = w_in.shape[0]

    cos2, sin2 = _rotary_tables(s)

    x2d = x.reshape(b * s, d)
    for l in range(depth):
        q4, kt4, v4, sg4, gates, u = _in_proj(
            x2d, norm1_w[l][None, :], w_in[l], cos2, sin2, b_glu[l][None, :],
            ret_gn_w[l][None, :], b, s)
        raw_b = jnp.broadcast_to(ret_decay_raw[l].T[:, :, None], (RET_HEADS, 2, RET_V_DIM))
        o4 = _retention(q4, kt4, v4, raw_b)
        csum, wro_b, wco_b, wout_b, w1_b, w2_b = _conv_sums(
            u, conv_w[l], conv_b[l], (w_ret_o[l], w_conv_o[l], w_out[l], w_mlp1[l], w_mlp2[l]),
            b, s)
        x2d = _merge(x2d, o4, sg4, csum, gates, conv_ln_w[l][None, :], conv_ln_b[l][None, :],
                     wro_b, wco_b, b_conv_o[l][None, :], wout_b)
        x2d = _mlp(x2d, norm2_w[l][None, :], w1_b, w2_b, norm_f_w[None, :],
                   final_norm=(l == depth - 1))
    return x2d.reshape(b, s, d)
```

```python
import functools

import jax
import jax.numpy as jnp
from jax import lax
from jax.experimental import pallas as pl
from jax.experimental.pallas import tpu as pltpu

F32 = jnp.float32
BF16 = jnp.bfloat16

RET_HEADS = 8
RET_QK_DIM = 128
RET_V_DIM = 256
CONV_WIDTH = 31
CONV_HALF = CONV_WIDTH // 2
ROPE_BASE = 10000.0
EPS = 1e-6
GN_EPS = 1e-5

LANES = 128
SUBLANES = 8
BF16_ROWS = 16
HALO = 16
RET_CHUNK = 256
VMEM_LIMIT = 56 * 1024 * 1024


def _sigmoid(x):
    return 0.5 * jnp.tanh(0.5 * x) + 0.5


def _rms_scale(x):
    return x * lax.rsqrt(jnp.mean(x * x, axis=-1, keepdims=True) + EPS)


def _const_spec(shape):
    return pl.BlockSpec(shape, lambda i: (0,) * len(shape), pipeline_mode=pl.Buffered(1))


def _slab_specs(weights, nsteps, index_map):
    specs = []
    for w in weights:
        rows = w.shape[0] // nsteps
        assert rows * nsteps == w.shape[0] and rows % BF16_ROWS == 0
        specs.append(pl.BlockSpec((rows, w.shape[1]), index_map))
    return specs


def _cast_weight_once(w_hbm, w_vmem, stage_ref, sem_ref):
    nslots, chunk_rows = stage_ref.shape[0], stage_ref.shape[1]
    nchunks = w_hbm.shape[0] // chunk_rows

    def copy(c):
        slot = c % nslots
        return pltpu.make_async_copy(
            w_hbm.at[pl.ds(c * chunk_rows, chunk_rows), :], stage_ref.at[slot], sem_ref.at[slot])

    for c in range(min(nslots - 1, nchunks)):
        copy(c).start()
    for c in range(nchunks):
        if c + nslots - 1 < nchunks:
            copy(c + nslots - 1).start()
        copy(c).wait()
        w_vmem[c * chunk_rows:(c + 1) * chunk_rows, :] = stage_ref[c % nslots].astype(BF16)


def _in_proj_kernel(x_ref, n1_ref, w_hbm, cos_ref, sin_ref, bglu_ref, gnw_ref,
                    q_ref, kt_ref, v_ref, sg_ref, gates_ref, u_ref, w_ref, stage_ref, sem_ref,
                    *, k_scale, nw, row_splits):
    @pl.when(pl.program_id(0) == 0)
    def _():
        _cast_weight_once(w_hbm, w_ref, stage_ref, sem_ref)

    tm = x_ref.shape[0]
    rows = tm // row_splits
    hv = nw // RET_V_DIM
    for r0 in range(0, tm, rows):
        rs = slice(r0, r0 + rows)
        h = (_rms_scale(x_ref[rs, :]) * n1_ref[...]).astype(BF16)

        def dots(group, h=h):
            lo = 2 * group * nw
            a = jnp.dot(h, w_ref[:, lo:lo + nw], preferred_element_type=F32)
            b = jnp.dot(h, w_ref[:, lo + nw:lo + 2 * nw], preferred_element_type=F32)
            return a, b

        a, b = dots(0)
        c = cos_ref[rs, :]
        s = sin_ref[rs, :]
        for hh in range(nw // RET_QK_DIM):
            t = a[:, hh * RET_QK_DIM:(hh + 1) * RET_QK_DIM]
            rot = pltpu.roll(t, RET_QK_DIM // 2, axis=1)
            q_ref[0, hh, rs, :] = (t * c + rot * s).astype(BF16)
        ck = c * k_scale
        sk = s * k_scale
        for hh in range(nw // RET_QK_DIM):
            t = b[:, hh * RET_QK_DIM:(hh + 1) * RET_QK_DIM]
            rot = pltpu.roll(t, RET_QK_DIM // 2, axis=1)
            kt_ref[0, hh, :, rs] = (t * ck + rot * sk).T.astype(BF16)

        for half, r in enumerate(dots(2)):
            y = (r * _sigmoid(r) * gnw_ref[:, half * nw:(half + 1) * nw]).astype(BF16)
            for hh in range(hv):
                sg_ref[0, half * hv + hh, rs, :] = y[:, hh * RET_V_DIM:(hh + 1) * RET_V_DIM]

        a, b = dots(3)
        u = (a + bglu_ref[:, :nw]) * _sigmoid(b + bglu_ref[:, nw:])
        for g in range(nw // LANES):
            u_ref[pl.ds(r0 * SUBLANES + g, rows, stride=SUBLANES), :] = (
                u[:, g * LANES:(g + 1) * LANES])

        a, b = dots(4)
        gates_ref[rs, :nw] = _sigmoid(a).astype(BF16)
        gates_ref[rs, nw:] = _sigmoid(b).astype(BF16)

        for half, r in enumerate(dots(1)):
            y = r.astype(BF16)
            for hh in range(hv):
                v_ref[0, half * hv + hh, rs, :] = y[:, hh * RET_V_DIM:(hh + 1) * RET_V_DIM]


def _in_proj(x2d, n1, w_in, cos2, sin2, b_glu, gnw, batch, seq, tm=512):
    m, d = x2d.shape
    sblocks = seq // tm
    nw = 1024
    h = RET_HEADS
    assert nw // LANES == SUBLANES and w_in.shape[1] == 10 * nw
    assert nw == h * RET_QK_DIM and 2 * nw == h * RET_V_DIM

    def head_spec(width):
        return pl.BlockSpec((1, h, tm, width), lambda i: (i // sblocks, 0, i % sblocks, 0))

    return pl.pallas_call(
        functools.partial(_in_proj_kernel, k_scale=RET_QK_DIM ** -0.5, nw=nw, row_splits=2),
        grid=(m // tm,),
        in_specs=[
            pl.BlockSpec((tm, d), lambda i: (i, 0)),
            _const_spec((1, d)),
            pl.BlockSpec(memory_space=pl.ANY),
            pl.BlockSpec((tm, RET_QK_DIM), lambda i: (i % sblocks, 0)),
            pl.BlockSpec((tm, RET_QK_DIM), lambda i: (i % sblocks, 0)),
            _const_spec((1, 2 * nw)),
            _const_spec((1, 2 * nw)),
        ],
        out_specs=[
            head_spec(RET_QK_DIM),
            pl.BlockSpec((1, h, RET_QK_DIM, tm), lambda i: (i // sblocks, 0, 0, i % sblocks)),
            head_spec(RET_V_DIM),
            head_spec(RET_V_DIM),
            pl.BlockSpec((tm, 2 * nw), lambda i: (i, 0)),
            pl.BlockSpec((tm * SUBLANES, LANES), lambda i: (i, 0)),
        ],
        out_shape=[
            jax.ShapeDtypeStruct((batch, h, seq, RET_QK_DIM), BF16),
            jax.ShapeDtypeStruct((batch, h, RET_QK_DIM, seq), BF16),
            jax.ShapeDtypeStruct((batch, h, seq, RET_V_DIM), BF16),
            jax.ShapeDtypeStruct((batch, h, seq, RET_V_DIM), BF16),
            jax.ShapeDtypeStruct((m, 2 * nw), BF16),
            jax.ShapeDtypeStruct((m * SUBLANES, LANES), F32),
        ],
        scratch_shapes=[
            pltpu.VMEM((d, 10 * nw), BF16),
            pltpu.VMEM((8, 16, 10 * nw), F32),
            pltpu.SemaphoreType.DMA((8,)),
        ],
        compiler_params=pltpu.CompilerParams(
            dimension_semantics=("arbitrary",),
            vmem_limit_bytes=VMEM_LIMIT),
        name="in_proj",
    )(x2d, n1, w_in, cos2, sin2, b_glu, gnw)


def _ret_kernel(raw_ref, q_ref, kt_ref, v_ref, o_ref, rst_ref, d_ref, zeta_ref, xi_ref):
    c = RET_CHUNK
    dk = RET_QK_DIM
    nchunks = q_ref.shape[2] // c
    lg = -jnp.exp(raw_ref[0])
    lgf = lg[0:1, :]
    lgb = lg[1:2, :]

    row = lax.broadcasted_iota(jnp.int32, (c, c), 0)
    col = lax.broadcasted_iota(jnp.int32, (c, c), 1)
    dpos = jnp.maximum(row - col, 0).astype(F32)
    dneg = jnp.maximum(col - row, 0).astype(F32)
    d_ref[...] = jnp.where(row >= col, jnp.exp(lgf * dpos), jnp.exp(lgb * dneg))

    pos = lax.broadcasted_iota(jnp.int32, (c, dk), 0).astype(F32)
    post = lax.broadcasted_iota(jnp.int32, (dk, c), 1).astype(F32)
    lgf_k = lgf[:, :dk]
    lgb_k = lgb[:, :dk]
    zeta_ref[0] = jnp.exp(lgf * (c - 1.0 - post)).astype(BF16)
    zeta_ref[1] = jnp.exp(lgb * post).astype(BF16)
    xi_ref[:, :dk] = jnp.exp(lgf_k * (pos + 1.0)).astype(BF16)
    xi_ref[:, dk:] = jnp.exp(lgb_k * (c - pos)).astype(BF16)
    g_f = jnp.exp(lgf * c)
    g_b = jnp.exp(lgb * c)

    def state_body(t, carry):
        rf, rb = carry
        nf = t
        nb = nchunks - 1 - t
        rst_ref[nf, 0:dk, :] = rf.astype(BF16)
        rst_ref[nb, dk:, :] = rb.astype(BF16)
        sf = pl.ds(pl.multiple_of(nf * c, c), c)
        sb = pl.ds(pl.multiple_of(nb * c, c), c)
        kzf = kt_ref[0, 0, :, sf] * zeta_ref[0]
        kzb = kt_ref[0, 0, :, sb] * zeta_ref[1]
        uf = jnp.dot(kzf, v_ref[0, 0, sf, :], preferred_element_type=F32)
        ub = jnp.dot(kzb, v_ref[0, 0, sb, :], preferred_element_type=F32)
        return g_f * rf + uf, g_b * rb + ub

    zero = jnp.zeros((dk, RET_V_DIM), F32)
    lax.fori_loop(0, nchunks, state_body, (zero, zero), unroll=True)

    def out_body(n, _):
        sl = pl.ds(pl.multiple_of(n * c, c), c)
        qc = q_ref[0, 0, sl, :]
        s = jnp.dot(qc, kt_ref[0, 0, :, sl], preferred_element_type=F32)
        p = (s * d_ref[...]).astype(BF16)
        qx = jnp.concatenate([qc, qc], axis=1) * xi_ref[...]
        o = (jnp.dot(p, v_ref[0, 0, sl, :], preferred_element_type=F32)
             + jnp.dot(qx, rst_ref[n], preferred_element_type=F32))
        mu = jnp.mean(o, axis=-1, keepdims=True)
        oc = o - mu
        var = jnp.mean(oc * oc, axis=-1, keepdims=True)
        o_ref[0, 0, sl, :] = (oc * lax.rsqrt(var + GN_EPS)).astype(BF16)
        return 0

    lax.fori_loop(0, nchunks, out_body, 0, unroll=True)


def _retention(q4, kt4, v4, raw_b):
    b, h, s, _ = q4.shape
    nchunks = s // RET_CHUNK

    def head_spec(width):
        return pl.BlockSpec((1, 1, s, width), lambda bi, hi: (bi, hi, 0, 0))

    return pl.pallas_call(
        _ret_kernel,
        grid=(b, h),
        in_specs=[
            pl.BlockSpec((1, 2, RET_V_DIM), lambda bi, hi: (hi, 0, 0)),
            head_spec(RET_QK_DIM),
            pl.BlockSpec((1, 1, RET_QK_DIM, s), lambda bi, hi: (bi, hi, 0, 0)),
            head_spec(RET_V_DIM),
        ],
        out_specs=head_spec(RET_V_DIM),
        out_shape=jax.ShapeDtypeStruct((b, h, s, RET_V_DIM), BF16),
        scratch_shapes=[
            pltpu.VMEM((nchunks, 2 * RET_QK_DIM, RET_V_DIM), BF16),
            pltpu.VMEM((RET_CHUNK, RET_CHUNK), F32),
            pltpu.VMEM((2, RET_QK_DIM, RET_CHUNK), BF16),
            pltpu.VMEM((RET_CHUNK, 2 * RET_QK_DIM), BF16),
        ],
        compiler_params=pltpu.CompilerParams(
            dimension_semantics=("arbitrary", "arbitrary"),
            vmem_limit_bytes=VMEM_LIMIT),
        name="retention",
    )(raw_b, q4, kt4, v4)


def _conv_kernel(prev_ref, main_ref, next_ref, cw_ref, cb_ref, *rest, tblk, n_cast):
    cast_in, out_ref, cast_out, edge_ref = (rest[:n_cast], rest[n_cast], rest[n_cast + 1:-1],
                                             rest[-1])
    for src, dst in zip(cast_in, cast_out):
        dst[...] = src[...].astype(BF16)

    i = pl.program_id(1)
    last = pl.num_programs(1) - 1
    sl = SUBLANES
    ts = main_ref.shape[0] // sl
    span = tblk + CONV_WIDTH - 1
    nblocks = ts // tblk
    assert span - tblk <= 2 * HALO and nblocks >= 3

    w = [cw_ref[k * sl:(k + 1) * sl, :] for k in range(CONV_WIDTH)]
    bias = cb_ref[...]

    def conv_block(src_ref, src_tok0, out_tok0):
        accs = [bias] * tblk
        for m in range(span):
            start = (src_tok0 + m) * sl
            if not isinstance(start, int):
                start = pl.multiple_of(start, sl)
            tok = src_ref[pl.ds(start, sl), :]
            for tt in range(tblk):
                k = m - tt
                if 0 <= k < CONV_WIDTH:
                    accs[tt] = accs[tt] + tok * w[k]
        for tt in range(tblk):
            o = (out_tok0 + tt) * sl
            if not isinstance(o, int):
                o = pl.multiple_of(o, sl)
            out_ref[pl.ds(o, sl), :] = accs[tt]

    edge_ref[0:HALO * sl, :] = jnp.where(i > 0, prev_ref[...], 0.0)
    edge_ref[HALO * sl:(HALO + span) * sl, :] = main_ref[0:span * sl, :]
    conv_block(edge_ref, HALO - CONV_HALF, 0)
    e1 = HALO + span
    edge_ref[e1 * sl:(e1 + span) * sl, :] = main_ref[(ts - span) * sl:, :]
    edge_ref[(e1 + span) * sl:, :] = jnp.where(i < last, next_ref[...], 0.0)
    conv_block(edge_ref, e1 + span - tblk - CONV_HALF, ts - tblk)

    def conv_body(bi, _):
        t0 = bi * tblk
        conv_block(main_ref, t0 - CONV_HALF, t0)
        return 0

    lax.fori_loop(1, nblocks - 1, conv_body, 0)


def _conv_sums(u_tm, conv_w, conv_b, cast_weights, batch, seq, ts=1024, tblk=16):
    sl = SUBLANES
    nblk = seq // ts
    hb = ts // HALO
    nhalo = batch * seq // HALO
    cw = conv_w.reshape(CONV_WIDTH * sl, LANES)
    cb = conv_b.reshape(sl, LANES)
    slabs = _slab_specs(cast_weights, batch * nblk, lambda bi, i: (bi * nblk + i, 0))
    return pl.pallas_call(
        functools.partial(_conv_kernel, tblk=tblk, n_cast=len(cast_weights)),
        grid=(batch, nblk),
        in_specs=[
            pl.BlockSpec((HALO * sl, LANES),
                         lambda bi, i: (jnp.maximum((bi * nblk + i) * hb - 1, 0), 0)),
            pl.BlockSpec((ts * sl, LANES), lambda bi, i: (bi * nblk + i, 0)),
            pl.BlockSpec((HALO * sl, LANES),
                         lambda bi, i: (jnp.minimum((bi * nblk + i + 1) * hb, nhalo - 1), 0)),
            pl.BlockSpec((CONV_WIDTH * sl, LANES), lambda bi, i: (0, 0)),
            pl.BlockSpec((sl, LANES), lambda bi, i: (0, 0)),
        ] + slabs,
        out_specs=[pl.BlockSpec((ts * sl, LANES), lambda bi, i: (bi * nblk + i, 0))] + slabs,
        out_shape=[jax.ShapeDtypeStruct(u_tm.shape, F32)]
        + [jax.ShapeDtypeStruct(w.shape, BF16) for w in cast_weights],
        scratch_shapes=[pltpu.VMEM((2 * (HALO + tblk + CONV_WIDTH - 1) * sl, LANES), F32)],
        compiler_params=pltpu.CompilerParams(
            dimension_semantics=("arbitrary", "arbitrary"),
            vmem_limit_bytes=VMEM_LIMIT),
        name="conv_sums",
    )(u_tm, u_tm, u_tm, cw, cb, *cast_weights)


def _merge_kernel(x_ref, o_ref, sg_ref, cs_ref, gr_ref, gc_ref, lnw_ref, lnb_ref, wro_ref, wco_ref,
                  bco_ref, wout_ref, x2_ref, *, row_splits):
    tm = x_ref.shape[0]
    rows = tm // row_splits
    sl = SUBLANES
    for r0 in range(0, tm, rows):
        rs = slice(r0, r0 + rows)
        og = jnp.concatenate([o_ref[0, hh, rs, :] * sg_ref[0, hh, rs, :]
                              for hh in range(o_ref.shape[1])], axis=1)
        y_ret = jnp.dot(og, wro_ref[...], preferred_element_type=F32)
        a = jnp.concatenate(
            [cs_ref[pl.ds(r0 * sl + g, rows, stride=sl), :] for g in range(sl)], axis=1)
        mu = jnp.mean(a, axis=-1, keepdims=True)
        ac = a - mu
        var = jnp.mean(ac * ac, axis=-1, keepdims=True)
        y = ac * lax.rsqrt(var + GN_EPS) * lnw_ref[...] + lnb_ref[...]
        uc = (y * _sigmoid(y)).astype(BF16)
        y_conv = jnp.dot(uc, wco_ref[...], preferred_element_type=F32) + bco_ref[...]
        m = gr_ref[rs, :].astype(F32) * y_ret + gc_ref[rs, :].astype(F32) * y_conv
        x2_ref[rs, :] = x_ref[rs, :] + jnp.dot(m.astype(BF16), wout_ref[...],
                                               preferred_element_type=F32)


def _merge(x2d, o4, sg4, csum_tm, gates, ln_w, ln_b, w_ret_o, w_conv_o, b_conv_o, w_out,
           tm=512):
    m, d = x2d.shape
    _, h, seq, dv = o4.shape
    vw = h * dv
    sblocks = seq // tm
    return pl.pallas_call(
        functools.partial(_merge_kernel, row_splits=2),
        grid=(m // tm,),
        in_specs=[
            pl.BlockSpec((tm, d), lambda i: (i, 0)),
            pl.BlockSpec((1, h, tm, dv), lambda i: (i // sblocks, 0, i % sblocks, 0)),
            pl.BlockSpec((1, h, tm, dv), lambda i: (i // sblocks, 0, i % sblocks, 0)),
            pl.BlockSpec((tm * SUBLANES, LANES), lambda i: (i, 0)),
            pl.BlockSpec((tm, d), lambda i: (i, 0)),
            pl.BlockSpec((tm, d), lambda i: (i, 1)),
            _const_spec((1, d)),
            _const_spec((1, d)),
            _const_spec((vw, d)),
            _const_spec((d, d)),
            _const_spec((1, d)),
            _const_spec((d, d)),
        ],
        out_specs=pl.BlockSpec((tm, d), lambda i: (i, 0)),
        out_shape=jax.ShapeDtypeStruct((m, d), F32),
        compiler_params=pltpu.CompilerParams(
            dimension_semantics=("arbitrary",),
            vmem_limit_bytes=VMEM_LIMIT),
        name="merge",
    )(x2d, o4, sg4, csum_tm, gates, gates, ln_w, ln_b, w_ret_o, w_conv_o, b_conv_o, w_out)


def _mlp_kernel(x_ref, n2_ref, w1_ref, w2_ref, nf_ref, out_ref, *, final_norm, ff_chunk,
                row_splits):
    tm = x_ref.shape[0]
    rows = tm // row_splits
    for r0 in range(0, tm, rows):
        x = x_ref[r0:r0 + rows, :]
        h = (_rms_scale(x) * n2_ref[...]).astype(BF16)
        acc = x
        for c0 in range(0, w1_ref.shape[1], ff_chunk):
            a = jnp.dot(h, w1_ref[:, c0:c0 + ff_chunk], preferred_element_type=F32)
            a = jnp.square(jnp.maximum(a, 0.0)).astype(BF16)
            acc = acc + jnp.dot(a, w2_ref[c0:c0 + ff_chunk, :], preferred_element_type=F32)
        if final_norm:
            acc = _rms_scale(acc) * nf_ref[...]
        out_ref[r0:r0 + rows, :] = acc


def _mlp(x2d, n2, w1, w2, nf, final_norm, tm=1024):
    m, d = x2d.shape
    dff = w1.shape[1]
    return pl.pallas_call(
        functools.partial(_mlp_kernel, final_norm=final_norm, ff_chunk=1024, row_splits=4),
        grid=(m // tm,),
        in_specs=[
            pl.BlockSpec((tm, d), lambda i: (i, 0)),
            _const_spec((1, d)),
            _const_spec((d, dff)),
            _const_spec((dff, d)),
            _const_spec((1, d)),
        ],
        out_specs=pl.BlockSpec((tm, d), lambda i: (i, 0)),
        out_shape=jax.ShapeDtypeStruct((m, d), F32),
        compiler_params=pltpu.CompilerParams(
            dimension_semantics=("arbitrary",),
            vmem_limit_bytes=VMEM_LIMIT),
        name="mlp",
    )(x2d, n2, w1, w2, nf)


def _rotary_tables(seq, block=64):
    half = RET_QK_DIM // 2
    lane = jnp.arange(RET_QK_DIM)
    inv_freq = ROPE_BASE ** (-(2 * (lane % half)).astype(F32) / RET_QK_DIM)
    sign = jnp.where(lane < half, -1.0, 1.0).astype(F32)
    ang_hi = (jnp.arange(seq // block, dtype=F32) * block)[:, None] * inv_freq[None, :]
    ang_lo = jnp.arange(block, dtype=F32)[:, None] * inv_freq[None, :]
    ch, sh = jnp.cos(ang_hi)[:, None, :], jnp.sin(ang_hi)[:, None, :]
    cl, sl = jnp.cos(ang_lo)[None, :, :], jnp.sin(ang_lo)[None, :, :]
    cos2 = (ch * cl - sh * sl).reshape(seq, RET_QK_DIM)
    sin2 = ((sh * cl + ch * sl) * sign).reshape(seq, RET_QK_DIM)
    return cos2, sin2


def kernel(x, norm1_w, w_in, ret_decay_raw, ret_gn_w, w_ret_o, b_glu, conv_w, conv_b, conv_ln_w,
           conv_ln_b, w_conv_o, b_conv_o, w_out, norm2_w, w_mlp1, w_mlp2, norm_f_w):
    b, s, d = x.shape
    depth = w_in.shape[0]

    cos2, sin2 = _rotary_tables(s)

    x2d = x.reshape(b * s, d)
    for l in range(depth):
        q4, kt4, v4, sg4, gates, u = _in_proj(
            x2d, norm1_w[l][None, :], w_in[l], cos2, sin2, b_glu[l][None, :],
            ret_gn_w[l][None, :], b, s)
        raw_b = jnp.broadcast_to(ret_decay_raw[l].T[:, :, None], (RET_HEADS, 2, RET_V_DIM))
        o4 = _retention(q4, kt4, v4, raw_b)
        csum, wro_b, wco_b, wout_b, w1_b, w2_b = _conv_sums(
            u, conv_w[l], conv_b[l], (w_ret_o[l], w_conv_o[l], w_out[l], w_mlp1[l], w_mlp2[l]),
            b, s)
        x2d = _merge(x2d, o4, sg4, csum, gates, conv_ln_w[l][None, :], conv_ln_b[l][None, :],
                     wro_b, wco_b, b_conv_o[l][None, :], wout_b)
        x2d = _mlp(x2d, norm2_w[l][None, :], w1_b, w2_b, norm_f_w[None, :],
                   final_norm=(l == depth - 1))
    return x2d.reshape(b, s, d)
```

```python
import functools

import jax
import jax.numpy as jnp
from jax import lax
from jax.experimental import pallas as pl
from jax.experimental.pallas import tpu as pltpu

F32 = jnp.float32
BF16 = jnp.bfloat16

RET_HEADS = 8
RET_QK_DIM = 128
RET_V_DIM = 256
CONV_WIDTH = 31
CONV_HALF = CONV_WIDTH // 2
ROPE_BASE = 10000.0
EPS = 1e-6
GN_EPS = 1e-5

LANES = 128
SUBLANES = 8
BF16_ROWS = 16
HALO = 16
RET_CHUNK = 256
VMEM_LIMIT = 56 * 1024 * 1024


def _sigmoid(x):
    return 0.5 * jnp.tanh(0.5 * x) + 0.5


def _rms_scale(x):
    return x * lax.rsqrt(jnp.mean(x * x, axis=-1, keepdims=True) + EPS)


def _const_spec(shape):
    return pl.BlockSpec(shape, lambda i: (0,) * len(shape), pipeline_mode=pl.Buffered(1))


def _slab_specs(weights, nsteps, index_map):
    specs = []
    for w in weights:
        rows = w.shape[0] // nsteps
        assert rows * nsteps == w.shape[0] and rows % BF16_ROWS == 0
        specs.append(pl.BlockSpec((rows, w.shape[1]), index_map))
    return specs


def _cast_weight_once(w_hbm, w_vmem, stage_ref, sem_ref):
    nslots, chunk_rows = stage_ref.shape[0], stage_ref.shape[1]
    nchunks = w_hbm.shape[0] // chunk_rows

    def copy(c):
        slot = c % nslots
        return pltpu.make_async_copy(
            w_hbm.at[pl.ds(c * chunk_rows, chunk_rows), :], stage_ref.at[slot], sem_ref.at[slot])

    for c in range(min(nslots - 1, nchunks)):
        copy(c).start()
    for c in range(nchunks):
        if c + nslots - 1 < nchunks:
            copy(c + nslots - 1).start()
        copy(c).wait()
        w_vmem[c * chunk_rows:(c + 1) * chunk_rows, :] = stage_ref[c % nslots].astype(BF16)


def _in_proj_kernel(x_ref, n1_ref, w_hbm, cos_ref, sin_ref, bglu_ref, gnw_ref,
                    q_ref, kt_ref, v_ref, sg_ref, gates_ref, u_ref, w_ref, stage_ref, sem_ref,
                    *, k_scale, nw, row_splits):
    @pl.when(pl.program_id(0) == 0)
    def _():
        _cast_weight_once(w_hbm, w_ref, stage_ref, sem_ref)

    tm = x_ref.shape[0]
    rows = tm // row_splits
    hv = nw // RET_V_DIM
    for r0 in range(0, tm, rows):
        rs = slice(r0, r0 + rows)
        h = (_rms_scale(x_ref[rs, :]) * n1_ref[...]).astype(BF16)

        def dots(group, h=h):
            lo = 2 * group * nw
            a = jnp.dot(h, w_ref[:, lo:lo + nw], preferred_element_type=F32)
            b = jnp.dot(h, w_ref[:, lo + nw:lo + 2 * nw], preferred_element_type=F32)
            return a, b

        a, b = dots(0)
        c = cos_ref[rs, :]
        s = sin_ref[rs, :]
        for hh in range(nw // RET_QK_DIM):
            t = a[:, hh * RET_QK_DIM:(hh + 1) * RET_QK_DIM]
            rot = pltpu.roll(t, RET_QK_DIM // 2, axis=1)
            q_ref[0, hh, rs, :] = (t * c + rot * s).astype(BF16)
        ck = c * k_scale
        sk = s * k_scale
        for hh in range(nw // RET_QK_DIM):
            t = b[:, hh * RET_QK_DIM:(hh + 1) * RET_QK_DIM]
            rot = pltpu.roll(t, RET_QK_DIM // 2, axis=1)
            kt_ref[0, hh, :, rs] = (t * ck + rot * sk).T.astype(BF16)

        for half, r in enumerate(dots(2)):
            y = (r * _sigmoid(r) * gnw_ref[:, half * nw:(half + 1) * nw]).astype(BF16)
            for hh in range(hv):
                sg_ref[0, half * hv + hh, rs, :] = y[:, hh * RET_V_DIM:(hh + 1) * RET_V_DIM]

        a, b = dots(3)
        u = (a + bglu_ref[:, :nw]) * _sigmoid(b + bglu_ref[:, nw:])
        for g in range(nw // LANES):
            u_ref[pl.ds(r0 * SUBLANES + g, rows, stride=SUBLANES), :] = (
                u[:, g * LANES:(g + 1) * LANES])

        a, b = dots(4)
        gates_ref[rs, :nw] = _sigmoid(a).astype(BF16)
        gates_ref[rs, nw:] = _sigmoid(b).astype(BF16)

        for half, r in enumerate(dots(1)):
            y = r.astype(BF16)
            for hh in range(hv):
                v_ref[0, half * hv + hh, rs, :] = y[:, hh * RET_V_DIM:(hh + 1) * RET_V_DIM]


def _in_proj(x2d, n1, w_in, cos2, sin2, b_glu, gnw, batch, seq, tm=512):
    m, d = x2d.shape
    sblocks = seq // tm
    nw = 1024
    h = RET_HEADS
    assert nw // LANES == SUBLANES and w_in.shape[1] == 10 * nw
    assert nw == h * RET_QK_DIM and 2 * nw == h * RET_V_DIM

    def head_spec(width):
        return pl.BlockSpec((1, h, tm, width), lambda i: (i // sblocks, 0, i % sblocks, 0))

    return pl.pallas_call(
        functools.partial(_in_proj_kernel, k_scale=RET_QK_DIM ** -0.5, nw=nw, row_splits=2),
        grid=(m // tm,),
        in_specs=[
            pl.BlockSpec((tm, d), lambda i: (i, 0)),
            _const_spec((1, d)),
            pl.BlockSpec(memory_space=pl.ANY),
            pl.BlockSpec((tm, RET_QK_DIM), lambda i: (i % sblocks, 0)),
            pl.BlockSpec((tm, RET_QK_DIM), lambda i: (i % sblocks, 0)),
            _const_spec((1, 2 * nw)),
            _const_spec((1, 2 * nw)),
        ],
        out_specs=[
            head_spec(RET_QK_DIM),
            pl.BlockSpec((1, h, RET_QK_DIM, tm), lambda i: (i // sblocks, 0, 0, i % sblocks)),
            head_spec(RET_V_DIM),
            head_spec(RET_V_DIM),
            pl.BlockSpec((tm, 2 * nw), lambda i: (i, 0)),
            pl.BlockSpec((tm * SUBLANES, LANES), lambda i: (i, 0)),
        ],
        out_shape=[
            jax.ShapeDtypeStruct((batch, h, seq, RET_QK_DIM), BF16),
            jax.ShapeDtypeStruct((batch, h, RET_QK_DIM, seq), BF16),
            jax.ShapeDtypeStruct((batch, h, seq, RET_V_DIM), BF16),
            jax.ShapeDtypeStruct((batch, h, seq, RET_V_DIM), BF16),
            jax.ShapeDtypeStruct((m, 2 * nw), BF16),
            jax.ShapeDtypeStruct((m * SUBLANES, LANES), F32),
        ],
        scratch_shapes=[
            pltpu.VMEM((d, 10 * nw), BF16),
            pltpu.VMEM((12, 16, 10 * nw), F32),
            pltpu.SemaphoreType.DMA((12,)),
        ],
        compiler_params=pltpu.CompilerParams(
            dimension_semantics=("arbitrary",),
            vmem_limit_bytes=VMEM_LIMIT),
        name="in_proj",
    )(x2d, n1, w_in, cos2, sin2, b_glu, gnw)


def _ret_kernel(raw_ref, q_ref, kt_ref, v_ref, o_ref, rst_ref, d_ref, zeta_ref, xi_ref):
    c = RET_CHUNK
    dk = RET_QK_DIM
    nchunks = q_ref.shape[2] // c
    lg = -jnp.exp(raw_ref[0])
    lgf = lg[0:1, :]
    lgb = lg[1:2, :]

    row = lax.broadcasted_iota(jnp.int32, (c, c), 0)
    col = lax.broadcasted_iota(jnp.int32, (c, c), 1)
    dpos = jnp.maximum(row - col, 0).astype(F32)
    dneg = jnp.maximum(col - row, 0).astype(F32)
    d_ref[...] = jnp.where(row >= col, jnp.exp(lgf * dpos), jnp.exp(lgb * dneg))

    pos = lax.broadcasted_iota(jnp.int32, (c, dk), 0).astype(F32)
    post = lax.broadcasted_iota(jnp.int32, (dk, c), 1).astype(F32)
    lgf_k = lgf[:, :dk]
    lgb_k = lgb[:, :dk]
    zeta_ref[0] = jnp.exp(lgf * (c - 1.0 - post)).astype(BF16)
    zeta_ref[1] = jnp.exp(lgb * post).astype(BF16)
    xi_ref[:, :dk] = jnp.exp(lgf_k * (pos + 1.0)).astype(BF16)
    xi_ref[:, dk:] = jnp.exp(lgb_k * (c - pos)).astype(BF16)
    g_f = jnp.exp(lgf * c)
    g_b = jnp.exp(lgb * c)

    def state_body(t, carry):
        rf, rb = carry
        nf = t
        nb = nchunks - 1 - t
        rst_ref[nf, 0:dk, :] = rf.astype(BF16)
        rst_ref[nb, dk:, :] = rb.astype(BF16)
        sf = pl.ds(pl.multiple_of(nf * c, c), c)
        sb = pl.ds(pl.multiple_of(nb * c, c), c)
        kzf = kt_ref[0, 0, :, sf] * zeta_ref[0]
        kzb = kt_ref[0, 0, :, sb] * zeta_ref[1]
        uf = jnp.dot(kzf, v_ref[0, 0, sf, :], preferred_element_type=F32)
        ub = jnp.dot(kzb, v_ref[0, 0, sb, :], preferred_element_type=F32)
        return g_f * rf + uf, g_b * rb + ub

    zero = jnp.zeros((dk, RET_V_DIM), F32)
    lax.fori_loop(0, nchunks, state_body, (zero, zero), unroll=True)

    def out_body(n, _):
        sl = pl.ds(pl.multiple_of(n * c, c), c)
        qc = q_ref[0, 0, sl, :]
        s = jnp.dot(qc, kt_ref[0, 0, :, sl], preferred_element_type=F32)
        p = (s * d_ref[...]).astype(BF16)
        qx = jnp.concatenate([qc, qc], axis=1) * xi_ref[...]
        o = (jnp.dot(p, v_ref[0, 0, sl, :], preferred_element_type=F32)
             + jnp.dot(qx, rst_ref[n], preferred_element_type=F32))
        mu = jnp.mean(o, axis=-1, keepdims=True)
        oc = o - mu
        var = jnp.mean(oc * oc, axis=-1, keepdims=True)
        o_ref[0, 0, sl, :] = (oc * lax.rsqrt(var + GN_EPS)).astype(BF16)
        return 0

    lax.fori_loop(0, nchunks, out_body, 0, unroll=True)


def _retention(q4, kt4, v4, raw_b):
    b, h, s, _ = q4.shape
    nchunks = s // RET_CHUNK

    def head_spec(width):
        return pl.BlockSpec((1, 1, s, width), lambda bi, hi: (bi, hi, 0, 0))

    return pl.pallas_call(
        _ret_kernel,
        grid=(b, h),
        in_specs=[
            pl.BlockSpec((1, 2, RET_V_DIM), lambda bi, hi: (hi, 0, 0)),
            head_spec(RET_QK_DIM),
            pl.BlockSpec((1, 1, RET_QK_DIM, s), lambda bi, hi: (bi, hi, 0, 0)),
            head_spec(RET_V_DIM),
        ],
        out_specs=head_spec(RET_V_DIM),
        out_shape=jax.ShapeDtypeStruct((b, h, s, RET_V_DIM), BF16),
        scratch_shapes=[
            pltpu.VMEM((nchunks, 2 * RET_QK_DIM, RET_V_DIM), BF16),
            pltpu.VMEM((RET_CHUNK, RET_CHUNK), F32),
            pltpu.VMEM((2, RET_QK_DIM, RET_CHUNK), BF16),
            pltpu.VMEM((RET_CHUNK, 2 * RET_QK_DIM), BF16),
        ],
        compiler_params=pltpu.CompilerParams(
            dimension_semantics=("arbitrary", "arbitrary"),
            vmem_limit_bytes=VMEM_LIMIT),
        name="retention",
    )(raw_b, q4, kt4, v4)


def _conv_kernel(prev_ref, main_ref, next_ref, cw_ref, cb_ref, *rest, tblk, n_cast):
    cast_in, out_ref, cast_out, edge_ref = (rest[:n_cast], rest[n_cast], rest[n_cast + 1:-1],
                                             rest[-1])
    for src, dst in zip(cast_in, cast_out):
        dst[...] = src[...].astype(BF16)

    i = pl.program_id(1)
    last = pl.num_programs(1) - 1
    sl = SUBLANES
    ts = main_ref.shape[0] // sl
    span = tblk + CONV_WIDTH - 1
    nblocks = ts // tblk
    assert span - tblk <= 2 * HALO and nblocks >= 3

    w = [cw_ref[k * sl:(k + 1) * sl, :] for k in range(CONV_WIDTH)]
    bias = cb_ref[...]

    def conv_block(src_ref, src_tok0, out_tok0):
        accs = [bias] * tblk
        for m in range(span):
            start = (src_tok0 + m) * sl
            if not isinstance(start, int):
                start = pl.multiple_of(start, sl)
            tok = src_ref[pl.ds(start, sl), :]
            for tt in range(tblk):
                k = m - tt
                if 0 <= k < CONV_WIDTH:
                    accs[tt] = accs[tt] + tok * w[k]
        for tt in range(tblk):
            o = (out_tok0 + tt) * sl
            if not isinstance(o, int):
                o = pl.multiple_of(o, sl)
            out_ref[pl.ds(o, sl), :] = accs[tt]

    edge_ref[0:HALO * sl, :] = jnp.where(i > 0, prev_ref[...], 0.0)
    edge_ref[HALO * sl:(HALO + span) * sl, :] = main_ref[0:span * sl, :]
    conv_block(edge_ref, HALO - CONV_HALF, 0)
    e1 = HALO + span
    edge_ref[e1 * sl:(e1 + span) * sl, :] = main_ref[(ts - span) * sl:, :]
    edge_ref[(e1 + span) * sl:, :] = jnp.where(i < last, next_ref[...], 0.0)
    conv_block(edge_ref, e1 + span - tblk - CONV_HALF, ts - tblk)

    def conv_body(bi, _):
        t0 = bi * tblk
        conv_block(main_ref, t0 - CONV_HALF, t0)
        return 0

    lax.fori_loop(1, nblocks - 1, conv_body, 0)


def _conv_sums(u_tm, conv_w, conv_b, cast_weights, batch, seq, ts=1024, tblk=16):
    sl = SUBLANES
    nblk = seq // ts
    hb = ts // HALO
    nhalo = batch * seq // HALO
    cw = conv_w.reshape(CONV_WIDTH * sl, LANES)
    cb = conv_b.reshape(sl, LANES)
    slabs = _slab_specs(cast_weights, batch * nblk, lambda bi, i: (bi * nblk + i, 0))
    return pl.pallas_call(
        functools.partial(_conv_kernel, tblk=tblk, n_cast=len(cast_weights)),
        grid=(batch, nblk),
        in_specs=[
            pl.BlockSpec((HALO * sl, LANES),
                         lambda bi, i: (jnp.maximum((bi * nblk + i) * hb - 1, 0), 0)),
            pl.BlockSpec((ts * sl, LANES), lambda bi, i: (bi * nblk + i, 0)),
            pl.BlockSpec((HALO * sl, LANES),
                         lambda bi, i: (jnp.minimum((bi * nblk + i + 1) * hb, nhalo - 1), 0)),
            pl.BlockSpec((CONV_WIDTH * sl, LANES), lambda bi, i: (0, 0)),
            pl.BlockSpec((sl, LANES), lambda bi, i: (0, 0)),
        ] + slabs,
        out_specs=[pl.BlockSpec((ts * sl, LANES), lambda bi, i: (bi * nblk + i, 0))] + slabs,
        out_shape=[jax.ShapeDtypeStruct(u_tm.shape, F32)]
        + [jax.ShapeDtypeStruct(w.shape, BF16) for w in cast_weights],
        scratch_shapes=[pltpu.VMEM((2 * (HALO + tblk + CONV_WIDTH - 1) * sl, LANES), F32)],
        compiler_params=pltpu.CompilerParams(
            dimension_semantics=("arbitrary", "arbitrary"),
            vmem_limit_bytes=VMEM_LIMIT),
        name="conv_sums",
    )(u_tm, u_tm, u_tm, cw, cb, *cast_weights)


def _merge_kernel(x_ref, o_ref, sg_ref, cs_ref, gr_ref, gc_ref, lnw_ref, lnb_ref, wro_ref, wco_ref,
                  bco_ref, wout_ref, x2_ref, *, row_splits):
    tm = x_ref.shape[0]
    rows = tm // row_splits
    sl = SUBLANES
    for r0 in range(0, tm, rows):
        rs = slice(r0, r0 + rows)
        og = jnp.concatenate([o_ref[0, hh, rs, :] * sg_ref[0, hh, rs, :]
                              for hh in range(o_ref.shape[1])], axis=1)
        y_ret = jnp.dot(og, wro_ref[...], preferred_element_type=F32)
        a = jnp.concatenate(
            [cs_ref[pl.ds(r0 * sl + g, rows, stride=sl), :] for g in range(sl)], axis=1)
        mu = jnp.mean(a, axis=-1, keepdims=True)
        ac = a - mu
        var = jnp.mean(ac * ac, axis=-1, keepdims=True)
        y = ac * lax.rsqrt(var + GN_EPS) * lnw_ref[...] + lnb_ref[...]
        uc = (y * _sigmoid(y)).astype(BF16)
        y_conv = jnp.dot(uc, wco_ref[...], preferred_element_type=F32) + bco_ref[...]
        m = gr_ref[rs, :].astype(F32) * y_ret + gc_ref[rs, :].astype(F32) * y_conv
        x2_ref[rs, :] = x_ref[rs, :] + jnp.dot(m.astype(BF16), wout_ref[...],
                                               preferred_element_type=F32)


def _merge(x2d, o4, sg4, csum_tm, gates, ln_w, ln_b, w_ret_o, w_conv_o, b_conv_o, w_out,
           tm=512):
    m, d = x2d.shape
    _, h, seq, dv = o4.shape
    vw = h * dv
    sblocks = seq // tm
    return pl.pallas_call(
        functools.partial(_merge_kernel, row_splits=2),
        grid=(m // tm,),
        in_specs=[
            pl.BlockSpec((tm, d), lambda i: (i, 0)),
            pl.BlockSpec((1, h, tm, dv), lambda i: (i // sblocks, 0, i % sblocks, 0)),
            pl.BlockSpec((1, h, tm, dv), lambda i: (i // sblocks, 0, i % sblocks, 0)),
            pl.BlockSpec((tm * SUBLANES, LANES), lambda i: (i, 0)),
            pl.BlockSpec((tm, d), lambda i: (i, 0)),
            pl.BlockSpec((tm, d), lambda i: (i, 1)),
            _const_spec((1, d)),
            _const_spec((1, d)),
            _const_spec((vw, d)),
            _const_spec((d, d)),
            _const_spec((1, d)),
            _const_spec((d, d)),
        ],
        out_specs=pl.BlockSpec((tm, d), lambda i: (i, 0)),
        out_shape=jax.ShapeDtypeStruct((m, d), F32),
        compiler_params=pltpu.CompilerParams(
            dimension_semantics=("arbitrary",),
            vmem_limit_bytes=VMEM_LIMIT),
        name="merge",
    )(x2d, o4, sg4, csum_tm, gates, gates, ln_w, ln_b, w_ret_o, w_conv_o, b_conv_o, w_out)


def _mlp_kernel(x_ref, n2_ref, w1_ref, w2_ref, nf_ref, out_ref, *, final_norm, ff_chunk,
                row_splits):
    tm = x_ref.shape[0]
    rows = tm // row_splits
    for r0 in range(0, tm, rows):
        x = x_ref[r0:r0 + rows, :]
        h = (_rms_scale(x) * n2_ref[...]).astype(BF16)
        acc = x
        for c0 in range(0, w1_ref.shape[1], ff_chunk):
            a = jnp.dot(h, w1_ref[:, c0:c0 + ff_chunk], preferred_element_type=F32)
            a = jnp.square(jnp.maximum(a, 0.0)).astype(BF16)
            acc = acc + jnp.dot(a, w2_ref[c0:c0 + ff_chunk, :], preferred_element_type=F32)
        if final_norm:
            acc = _rms_scale(acc) * nf_ref[...]
        out_ref[r0:r0 + rows, :] = acc


def _mlp(x2d, n2, w1, w2, nf, final_norm, tm=1024):
    m, d = x2d.shape
    dff = w1.shape[1]
    return pl.pallas_call(
        functools.partial(_mlp_kernel, final_norm=final_norm, ff_chunk=1024, row_splits=4),
        grid=(m // tm,),
        in_specs=[
            pl.BlockSpec((tm, d), lambda i: (i, 0)),
            _const_spec((1, d)),
            _const_spec((d, dff)),
            _const_spec((dff, d)),
            _const_spec((1, d)),
        ],
        out_specs=pl.BlockSpec((tm, d), lambda i: (i, 0)),
        out_shape=jax.ShapeDtypeStruct((m, d), F32),
        compiler_params=pltpu.CompilerParams(
            dimension_semantics=("arbitrary",),
            vmem_limit_bytes=VMEM_LIMIT),
        name="mlp",
    )(x2d, n2, w1, w2, nf)


def _rotary_tables(seq, block=64):
    half = RET_QK_DIM // 2
    lane = jnp.arange(RET_QK_DIM)
    inv_freq = ROPE_BASE ** (-(2 * (lane % half)).astype(F32) / RET_QK_DIM)
    sign = jnp.where(lane < half, -1.0, 1.0).astype(F32)
    ang_hi = (jnp.arange(seq // block, dtype=F32) * block)[:, None] * inv_freq[None, :]
    ang_lo = jnp.arange(block, dtype=F32)[:, None] * inv_freq[None, :]
    ch, sh = jnp.cos(ang_hi)[:, None, :], jnp.sin(ang_hi)[:, None, :]
    cl, sl = jnp.cos(ang_lo)[None, :, :], jnp.sin(ang_lo)[None, :, :]
    cos2 = (ch * cl - sh * sl).reshape(seq, RET_QK_DIM)
    sin2 = ((sh * cl + ch * sl) * sign).reshape(seq, RET_QK_DIM)
    return cos2, sin2


def kernel(x, norm1_w, w_in, ret_decay_raw, ret_gn_w, w_ret_o, b_glu, conv_w, conv_b, conv_ln_w,
           conv_ln_b, w_conv_o, b_conv_o, w_out, norm2_w, w_mlp1, w_mlp2, norm_f_w):
    b, s, d = x.shape
    depth = w_in.shape[0]

    cos2, sin2 = _rotary_tables(s)

    x2d = x.reshape(b * s, d)
    for l in range(depth):
        q4, kt4, v4, sg4, gates, u = _in_proj(
            x2d, norm1_w[l][None, :], w_in[l], cos2, sin2, b_glu[l][None, :],
            ret_gn_w[l][None, :], b, s)
        raw_b = jnp.broadcast_to(ret_decay_raw[l].T[:, :, None], (RET_HEADS, 2, RET_V_DIM))
        o4 = _retention(q4, kt4, v4, raw_b)
        csum, wro_b, wco_b, wout_b, w1_b, w2_b = _conv_sums(
            u, conv_w[l], conv_b[l], (w_ret_o[l], w_conv_o[l], w_out[l], w_mlp1[l], w_mlp2[l]),
            b, s)
        x2d = _merge(x2d, o4, sg4, csum, gates, conv_ln_w[l][None, :], conv_ln_b[l][None, :],
                     wro_b, wco_b, b_conv_o[l][None, :], wout_b)
        x2d = _mlp(x2d, norm2_w[l][None, :], w1_b, w2_b, norm_f_w[None, :],
                   final_norm=(l == depth - 1))
    return x2d.reshape(b, s, d)
```
